```python
import math, functools
import jax, jax.numpy as jnp
from jax import lax
import numpy as np

D_MODEL = 1024
BATCH = 8
SEQ = 2048
DEPTH = 1
DEC_BATCH = 128
DEC_SEQ = 8
PAST_LEN = 8192
PAGE_SIZE = 128

D_MIX = D_MODEL
D_DIFF = D_MIX // 2
D_RWKV = D_MIX - D_DIFF
DA_HEADS = 4
DA_V = D_DIFF // DA_HEADS
DA_QK = DA_V // 2
RW_HEAD = 64
RW_HEADS = D_RWKV // RW_HEAD
D_W_LORA = max(32, int(round(1.8 * D_MODEL ** 0.5 / 32)) * 32)
D_A_LORA = max(32, int(round(1.8 * D_MODEL ** 0.5 / 32)) * 32)
D_G_LORA = max(32, int(round(0.6 * D_MODEL ** 0.8 / 32)) * 32)
RW_COLS = 3 * D_RWKV + D_W_LORA + D_A_LORA + D_G_LORA
D_IN = 3 * D_DIFF + RW_COLS
D_FF = ((8 * D_MODEL // 3 + 127) // 128) * 128
CONV_W = 3
Q_BLOCK = 128
RMS_EPS = 1e-6
SUBLN_EPS = 1e-5
GN_EPS = 64e-5
NEG = -1e30

kernel_name = 'hybrid_diffattn_rwkv7_convffn_step'


def rmsnorm(x, g, eps):
    xf = x.astype(jnp.float32)
    y = xf * lax.rsqrt(jnp.mean(xf * xf, axis=-1, keepdims=True) + eps)
    return (y * g.astype(jnp.float32)).astype(x.dtype)


def diff_attn_prompt(q, k, v, lam):
    B, T = q.shape[0], q.shape[1]
    qf = q.astype(jnp.float32) * (DA_QK ** -0.5)
    kf = k.astype(jnp.float32)
    vf = v.astype(jnp.float32)
    k_pos = jnp.arange(T)

    def block(i):
        start = i * Q_BLOCK
        qb = lax.dynamic_slice_in_dim(qf, start, Q_BLOCK, axis=1)
        s = jnp.einsum('bqhmd,bkhmd->bhmqk', qb, kf)
        q_pos = start + jnp.arange(Q_BLOCK)
        s = jnp.where(k_pos[None, :] <= q_pos[:, None], s, NEG)
        p = jax.nn.softmax(s, axis=-1)
        attn = p[:, :, 0] - lam * p[:, :, 1]
        return jnp.einsum('bhqk,bkhd->bqhd', attn, vf)

    o = lax.map(block, jnp.arange(T // Q_BLOCK))
    return o.transpose(1, 0, 2, 3, 4).reshape(B, T, DA_HEADS, DA_V)


def _online_update(carry, s, v):
    m, l, acc = carry
    m_new = jnp.maximum(m, jnp.max(s, axis=-1))
    corr = jnp.exp(m - m_new)
    p = jnp.exp(s - m_new[..., None])
    l = l * corr + jnp.sum(p, axis=-1)
    acc = acc * corr[..., None] + jnp.einsum('bhmqk,bkhd->bhmqd', p, v)
    return (m_new, l, acc)


def diff_attn_sample(q, k, v, lam, cache_k, cache_v, page_table, layer):
    B, T = q.shape[0], q.shape[1]
    f32 = jnp.float32
    qf = q.astype(f32) * (DA_QK ** -0.5)

    def step(carry, phys):
        kp = cache_k[layer, phys].astype(f32).reshape(B, -1, DA_HEADS, 2, DA_QK)
        vp = cache_v[layer, phys].astype(f32)
        s = jnp.einsum('bqhmd,bkhmd->bhmqk', qf, kp)
        return _online_update(carry, s, vp), None

    init = (jnp.full((B, DA_HEADS, 2, T), NEG, f32),
            jnp.zeros((B, DA_HEADS, 2, T), f32),
            jnp.zeros((B, DA_HEADS, 2, T, DA_V), f32))
    carry, _ = lax.scan(step, init, page_table.T)
    s = jnp.einsum('bqhmd,bkhmd->bhmqk', qf, k.astype(f32))
    causal = jnp.tril(jnp.ones((T, T), dtype=bool))
    s = jnp.where(causal, s, NEG)
    m, l, acc = _online_update(carry, s, v.astype(f32))
    o = acc / l[..., None]
    o = o[:, :, 0] - lam * o[:, :, 1]
    return o.transpose(0, 2, 1, 3)


def wkv7_scan(S0, r, decay, k, v, a, b):
    def step(S, inp):
        r_t, w_t, k_t, v_t, a_t, b_t = inp
        Sa = jnp.einsum('bhvk,bhk->bhv', S, a_t)
        S = S * w_t[:, :, None, :] + Sa[..., None] * b_t[:, :, None, :] + v_t[..., None] * k_t[:, :, None, :]
        y = jnp.einsum('bhvk,bhk->bhv', S, r_t)
        return S, y

    xs = (jnp.moveaxis(r, 1, 0), jnp.moveaxis(decay, 1, 0), jnp.moveaxis(k, 1, 0),
          jnp.moveaxis(v, 1, 0), jnp.moveaxis(a, 1, 0), jnp.moveaxis(b, 1, 0))
    S_T, ys = lax.scan(step, S0, xs)
    return S_T, jnp.moveaxis(ys, 0, 1)


def rwkv7_mix(z, prev, S0, lw):
    B, T = z.shape[0], z.shape[1]
    f32 = jnp.float32
    z = z.astype(f32)
    zs = jnp.concatenate([prev[:, None, :].astype(f32), z[:, :-1]], axis=1)
    zm = z + (zs - z) * lw['rw_mu'].astype(f32)
    o1, o2, o3 = D_RWKV, 2 * D_RWKV, 3 * D_RWKV
    o4 = o3 + D_W_LORA
    o5 = o4 + D_A_LORA
    r, k, v = zm[..., :o1], zm[..., o1:o2], zm[..., o2:o3]
    xw, xa, xg = zm[..., o3:o4], zm[..., o4:o5], zm[..., o5:]
    w = -jax.nn.softplus(-(lw['rw_w0'] + jnp.tanh(xw) @ lw['rw_w_up'])) - 0.5
    a = jax.nn.sigmoid(lw['rw_a0'] + xa @ lw['rw_a_up'])
    g = jax.nn.sigmoid(xg) @ lw['rw_g_up']
    heads = lambda t: t.reshape(B, T, RW_HEADS, RW_HEAD)
    r, w, k, v, a = heads(r), heads(w), heads(k), heads(v), heads(a)
    kk = k * lw['rw_k_k'].reshape(RW_HEADS, RW_HEAD)
    kk = kk / jnp.maximum(jnp.sqrt(jnp.sum(kk * kk, axis=-1, keepdims=True)), 1e-12)
    k = k * (1.0 + (a - 1.0) * lw['rw_k_a'].reshape(RW_HEADS, RW_HEAD))
    decay = jnp.exp(-jnp.exp(w))
    S_T, o = wkv7_scan(S0.astype(f32), r, decay, k, v, -kk, kk * a)
    mean = jnp.mean(o, axis=-1, keepdims=True)
    var = jnp.mean(jnp.square(o - mean), axis=-1, keepdims=True)
    o = ((o - mean) * lax.rsqrt(var + GN_EPS)).reshape(B, T, D_RWKV) * lw['rw_ln_g'] + lw['rw_ln_b']
    bonus = jnp.sum(r * k * lw['rw_r_k'], axis=-1, keepdims=True) * v
    o = (o + bonus.reshape(B, T, D_RWKV)) * g
    return o, z[:, -1], S_T


def block_forward(x, lw, lam_init, attn_core, shift_prev, wkv_prev, conv_prev):
    B, T = x.shape[0], x.shape[1]
    f32 = jnp.float32
    h = rmsnorm(x, lw['norm_mix'], RMS_EPS)
    proj = h @ lw['w_in']
    q = proj[..., :D_DIFF].reshape(B, T, DA_HEADS, 2, DA_QK)
    k = proj[..., D_DIFF:2 * D_DIFF].reshape(B, T, DA_HEADS, 2, DA_QK)
    v = proj[..., 2 * D_DIFF:3 * D_DIFF].reshape(B, T, DA_HEADS, DA_V)
    lam = (jnp.exp(jnp.sum(lw['lam_q1'].astype(f32) * lw['lam_k1'].astype(f32)))
           - jnp.exp(jnp.sum(lw['lam_q2'].astype(f32) * lw['lam_k2'].astype(f32))) + lam_init)
    o_a = attn_core(q, k, v, lam)
    o_a = rmsnorm(o_a, lw['subln'], SUBLN_EPS) * (1.0 - lam_init)
    o_b, shift_new, wkv_new = rwkv7_mix(proj[..., 3 * D_DIFF:], shift_prev, wkv_prev, lw)
    mix = jnp.concatenate([o_a.reshape(B, T, D_DIFF), o_b], axis=-1).astype(x.dtype)
    x = x + mix @ lw['w_out']
    h = rmsnorm(x, lw['norm_ffn'], RMS_EPS)
    up = h @ lw['w_up']
    pad = jnp.concatenate([conv_prev.astype(up.dtype), up], axis=1)
    c = lw['conv_b']
    for j in range(CONV_W):
        c = c + pad[:, j:j + T] * lw['conv_w'][j]
    gate, val = c[..., :D_FF], c[..., D_FF:]
    x = x + (jax.nn.silu(gate) * val) @ lw['w_down']
    return (x, k.reshape(B, T, DA_HEADS, 2 * DA_QK), v, shift_new, wkv_new, pad[:, T:])


def setup_inputs(seed: int = 0) -> dict:
    key = jax.random.key(seed)
    ks = jax.random.split(key, 40)
    f32 = jnp.float32
    L = DEPTH
    n_pages = PAST_LEN // PAGE_SIZE
    n_used = DEC_BATCH * n_pages
    n_pool = n_used + max(1, n_used // 4)
    nrm = lambda kk, shape, s: jax.random.normal(kk, shape, f32) * s
    page_table = jax.random.permutation(ks[7], n_pool)[:n_used].reshape(DEC_BATCH, n_pages).astype(jnp.int32)
    return {
        'x_prompt': nrm(ks[0], (BATCH, SEQ, D_MODEL), 1.0),
        'x_sample': nrm(ks[1], (DEC_BATCH, DEC_SEQ, D_MODEL), 1.0),
        'cache_k': nrm(ks[2], (L, n_pool, PAGE_SIZE, DA_HEADS, 2 * DA_QK), 1.0),
        'cache_v': nrm(ks[3], (L, n_pool, PAGE_SIZE, DA_HEADS, DA_V), 1.0),
        'state_wkv': nrm(ks[4], (L, DEC_BATCH, RW_HEADS, RW_HEAD, RW_HEAD), 0.3),
        'state_shift': nrm(ks[5], (L, DEC_BATCH, RW_COLS), 1.0),
        'state_conv': nrm(ks[6], (L, DEC_BATCH, CONV_W - 1, 2 * D_FF), 1.0),
        'page_table': page_table,
        'norm_mix': 1.0 + nrm(ks[8], (L, D_MODEL), 0.02),
        'w_in': nrm(ks[9], (L, D_MODEL, D_IN), D_MODEL ** -0.5),
        'lam_q1': nrm(ks[10], (L, DA_QK), 0.1),
        'lam_k1': nrm(ks[11], (L, DA_QK), 0.1),
        'lam_q2': nrm(ks[12], (L, DA_QK), 0.1),
        'lam_k2': nrm(ks[13], (L, DA_QK), 0.1),
        'subln': 1.0 + nrm(ks[14], (L, DA_V), 0.02),
        'rw_mu': jax.random.uniform(ks[15], (L, RW_COLS), f32),
        'rw_w0': jax.random.uniform(ks[16], (L, D_RWKV), f32, -6.0, -1.0),
        'rw_w_up': nrm(ks[17], (L, D_W_LORA, D_RWKV), 0.1),
        'rw_a0': nrm(ks[18], (L, D_RWKV), 0.1),
        'rw_a_up': nrm(ks[19], (L, D_A_LORA, D_RWKV), 0.1),
        'rw_g_up': nrm(ks[20], (L, D_G_LORA, D_RWKV), D_G_LORA ** -0.5),
        'rw_k_k': 0.85 + nrm(ks[21], (L, D_RWKV), 0.02),
        'rw_k_a': 1.0 + nrm(ks[22], (L, D_RWKV), 0.02),
        'rw_r_k': nrm(ks[23], (L, RW_HEADS, RW_HEAD), 0.1),
        'rw_ln_g': 1.0 + nrm(ks[24], (L, D_RWKV), 0.02),
        'rw_ln_b': nrm(ks[25], (L, D_RWKV), 0.02),
        'w_out': nrm(ks[26], (L, D_MIX, D_MODEL), D_MIX ** -0.5),
        'norm_ffn': 1.0 + nrm(ks[27], (L, D_MODEL), 0.02),
        'w_up': nrm(ks[28], (L, D_MODEL, 2 * D_FF), D_MODEL ** -0.5),
        'conv_w': nrm(ks[29], (L, CONV_W, 2 * D_FF), CONV_W ** -0.5),
        'conv_b': nrm(ks[30], (L, 2 * D_FF), 0.01),
        'w_down': nrm(ks[31], (L, D_FF, D_MODEL), D_FF ** -0.5),
        'norm_final': 1.0 + nrm(ks[32], (D_MODEL,), 0.02),
    }


def reference(x_prompt, x_sample, cache_k, cache_v, state_wkv, state_shift, state_conv, page_table,
              norm_mix, w_in, lam_q1, lam_k1, lam_q2, lam_k2, subln, rw_mu, rw_w0, rw_w_up, rw_a0,
              rw_a_up, rw_g_up, rw_k_k, rw_k_a, rw_r_k, rw_ln_g, rw_ln_b, w_out, norm_ffn, w_up,
              conv_w, conv_b, w_down, norm_final):
    xp, xs = x_prompt, x_sample
    B = xp.shape[0]
    kp_l, vp_l, wp_l, sp_l, cp_l = [], [], [], [], []
    ks_l, vs_l, ws_l, ss_l, cs_l = [], [], [], [], []
    for layer in range(DEPTH):
        lw = {
            'norm_mix': norm_mix[layer], 'w_in': w_in[layer],
            'lam_q1': lam_q1[layer], 'lam_k1': lam_k1[layer],
            'lam_q2': lam_q2[layer], 'lam_k2': lam_k2[layer], 'subln': subln[layer],
            'rw_mu': rw_mu[layer], 'rw_w0': rw_w0[layer], 'rw_w_up': rw_w_up[layer],
            'rw_a0': rw_a0[layer], 'rw_a_up': rw_a_up[layer], 'rw_g_up': rw_g_up[layer],
            'rw_k_k': rw_k_k[layer], 'rw_k_a': rw_k_a[layer], 'rw_r_k': rw_r_k[layer],
            'rw_ln_g': rw_ln_g[layer], 'rw_ln_b': rw_ln_b[layer], 'w_out': w_out[layer],
            'norm_ffn': norm_ffn[layer], 'w_up': w_up[layer], 'conv_w': conv_w[layer],
            'conv_b': conv_b[layer], 'w_down': w_down[layer],
        }
        lam_init = 0.8 - 0.6 * math.exp(-0.3 * layer)
        xp, kp, vp, sp, wp, cp = block_forward(
            xp, lw, lam_init, diff_attn_prompt,
            jnp.zeros((B, RW_COLS), xp.dtype),
            jnp.zeros((B, RW_HEADS, RW_HEAD, RW_HEAD), jnp.float32),
            jnp.zeros((B, CONV_W - 1, 2 * D_FF), xp.dtype))
        sample_core = functools.partial(diff_attn_sample, cache_k=cache_k, cache_v=cache_v,
                                        page_table=page_table, layer=layer)
        xs, kS, vS, sS, wS, cS = block_forward(
            xs, lw, lam_init, sample_core, state_shift[layer], state_wkv[layer], state_conv[layer])
        kp_l.append(kp); vp_l.append(vp); wp_l.append(wp); sp_l.append(sp); cp_l.append(cp)
        ks_l.append(kS); vs_l.append(vS); ws_l.append(wS); ss_l.append(sS); cs_l.append(cS)
    y_prompt = rmsnorm(xp, norm_final, RMS_EPS)
    y_sample = rmsnorm(xs, norm_final, RMS_EPS)
    k_prompt, v_prompt = jnp.stack(kp_l), jnp.stack(vp_l)
    wkv_prompt, shift_prompt, conv_prompt = jnp.stack(wp_l), jnp.stack(sp_l), jnp.stack(cp_l)
    k_sample, v_sample = jnp.stack(ks_l), jnp.stack(vs_l)
    wkv_sample, shift_sample, conv_sample = jnp.stack(ws_l), jnp.stack(ss_l), jnp.stack(cs_l)
    return (y_prompt, y_sample, k_prompt, v_prompt, wkv_prompt, shift_prompt, conv_prompt,
            k_sample, v_sample, wkv_sample, shift_sample, conv_sample)
```

```python
import functools
import math

import jax
import jax.numpy as jnp
from jax import lax
from jax.experimental import pallas as pl
from jax.experimental.pallas import tpu as pltpu

F32 = jnp.float32
BF16 = jnp.bfloat16

LANES = 128
SUBLANES = 8
VMEM_LIMIT_BYTES = 56 * 1024 * 1024

RMS_EPS = 1e-6
SUBLN_EPS = 1e-5
GN_EPS = 64e-5
NEG = -1e30

DA_HEADS = 4
DA_V = 128
DA_QK = 64
RW_HEAD = 64
RW_PAIR = 2 * RW_HEAD
CONV_W = 3


def _params(*sem):
    return pltpu.CompilerParams(dimension_semantics=sem, vmem_limit_bytes=VMEM_LIMIT_BYTES)


def _const_spec(shape):
    zeros = (0,) * len(shape)
    return pl.BlockSpec(shape, lambda *_: zeros, pipeline_mode=pl.Buffered(1))


def _dot(a, b):
    return jnp.dot(a, b, preferred_element_type=F32)


def _dot_nt(a, b):
    return lax.dot_general(a, b, (((1,), (1,)), ((), ())), preferred_element_type=F32)


def _split2(x):
    hi = x.astype(BF16)
    lo = (x - hi.astype(F32)).astype(BF16)
    return hi, lo


def _split3(x):
    hi = x.astype(BF16)
    r1 = x - hi.astype(F32)
    mid = r1.astype(BF16)
    lo = (r1 - mid.astype(F32)).astype(BF16)
    return hi, mid, lo


def _rmsnorm(x, g, eps):
    return x * lax.rsqrt(jnp.mean(x * x, axis=-1, keepdims=True) + eps) * g


def _sigmoid(x):
    return 1.0 / (1.0 + jnp.exp(-x))


def _norm_proj_kernel(x_ref, g_ref, wa_ref, wr_ref, wl_ref,
                      q_ref, k_ref, v_ref, kb_ref, vb_ref, zm_ref, zl_ref, *, d_diff):
    h = _rmsnorm(x_ref[...], g_ref[...], RMS_EPS).astype(BF16)
    pa = _dot(h, wa_ref[...])
    q_ref[...] = pa[:, :d_diff] * (DA_QK ** -0.5)
    k = pa[:, d_diff:2 * d_diff]
    v = pa[:, 2 * d_diff:]
    k_ref[...] = k
    v_ref[...] = v
    kb_ref[...] = k.astype(BF16)
    vb_ref[...] = v.astype(BF16)
    zm_ref[...] = _dot(h, wr_ref[...])
    zl_ref[...] = _dot(h, wl_ref[...])


def _norm_proj(x, g, wa, wr, wl, *, tm):
    n, d = x.shape
    d_diff = wa.shape[1] // 3
    n_rkv, n_lora = wr.shape[1], wl.shape[1]
    row = lambda w: pl.BlockSpec((tm, w), lambda i: (i, 0))
    return pl.pallas_call(
        functools.partial(_norm_proj_kernel, d_diff=d_diff),
        grid=(n // tm,),
        in_specs=[row(d), _const_spec((1, d)), _const_spec(wa.shape), _const_spec(wr.shape),
                  _const_spec(wl.shape)],
        out_specs=[row(d_diff), row(d_diff), row(d_diff), row(d_diff), row(d_diff),
                   row(n_rkv), row(n_lora)],
        out_shape=[jax.ShapeDtypeStruct((n, d_diff), F32),
                   jax.ShapeDtypeStruct((n, d_diff), F32),
                   jax.ShapeDtypeStruct((n, d_diff), F32),
                   jax.ShapeDtypeStruct((n, d_diff), BF16),
                   jax.ShapeDtypeStruct((n, d_diff), BF16),
                   jax.ShapeDtypeStruct((n, n_rkv), F32),
                   jax.ShapeDtypeStruct((n, n_lora), F32)],
        compiler_params=_params("arbitrary"),
        name="norm_proj",
    )(x, g, wa, wr, wl)


def _lam_value(lamv_ref, lam_init):
    lv = lamv_ref[...]
    s1 = jnp.sum(lv[0:1] * lv[1:2], axis=-1, keepdims=True)
    s2 = jnp.sum(lv[2:3] * lv[3:4], axis=-1, keepdims=True)
    return jnp.exp(s1) - jnp.exp(s2) + lam_init


def _stack_maps(q):
    lane = lax.broadcasted_iota(jnp.int32, q.shape, 1)
    zero = jnp.zeros_like(q)
    return jnp.concatenate([jnp.where(lane < DA_QK, q, zero),
                            jnp.where(lane >= DA_QK, q, zero)], axis=0)


def _attn_prompt_kernel(lamv_ref, subln_ref, q_ref, k_ref, v_ref, o_ref, *, tq, lam_init):
    i = pl.program_id(2)
    qs = _stack_maps(q_ref[...]).astype(BF16)

    def update(carry, kj, vj, mask):
        m, l, acc = carry
        s = _dot_nt(qs, kj)
        if mask is not None:
            s = jnp.where(mask, s, NEG)
        m_new = jnp.maximum(m, jnp.max(s, axis=-1, keepdims=True))
        corr = jnp.exp(m - m_new)
        p = jnp.exp(s - m_new)
        l = l * corr + jnp.sum(p, axis=-1, keepdims=True)
        acc = acc * corr + _dot(p.astype(BF16), vj)
        return m_new, l, acc

    def body(j, carry):
        start = pl.multiple_of(j * tq, tq)
        return update(carry, k_ref[pl.ds(start, tq), :], v_ref[pl.ds(start, tq), :], None)

    init = (jnp.full((2 * tq, 1), NEG, F32), jnp.zeros((2 * tq, 1), F32),
            jnp.zeros((2 * tq, DA_V), F32))
    carry = lax.fori_loop(0, i, body, init)
    start = pl.multiple_of(i * tq, tq)
    row = lax.broadcasted_iota(jnp.int32, (2 * tq, tq), 0)
    col = lax.broadcasted_iota(jnp.int32, (2 * tq, tq), 1)
    causal = col <= jnp.where(row >= tq, row - tq, row)
    m, l, acc = update(carry, k_ref[pl.ds(start, tq), :], v_ref[pl.ds(start, tq), :], causal)
    o = acc / l
    od = o[:tq] - _lam_value(lamv_ref, lam_init) * o[tq:]
    o_ref[...] = (_rmsnorm(od, subln_ref[...], SUBLN_EPS) * (1.0 - lam_init)).astype(o_ref.dtype)


def _attn_prompt(lamv, subln, q, kb, vb, *, batch, seq, tq, lam_init):
    n, d_diff = q.shape
    heads = d_diff // DA_V
    nq = seq // tq
    kb3 = kb.reshape(batch, seq, d_diff)
    vb3 = vb.reshape(batch, seq, d_diff)
    qspec = pl.BlockSpec((tq, DA_V), lambda b, h, i: (b * nq + i, h))
    kvspec = pl.BlockSpec((None, seq, DA_V), lambda b, h, i: (b, 0, h))
    return pl.pallas_call(
        functools.partial(_attn_prompt_kernel, tq=tq, lam_init=lam_init),
        grid=(batch, heads, nq),
        in_specs=[_const_spec(lamv.shape), _const_spec(subln.shape), qspec, kvspec, kvspec],
        out_specs=qspec,
        out_shape=jax.ShapeDtypeStruct((n, d_diff), BF16),
        compiler_params=_params("arbitrary", "arbitrary", "arbitrary"),
        name="attn_prompt",
    )(lamv, subln, q, kb3, vb3)


def _attn_paged_kernel(pt_ref, lamv_ref, subln_ref, q_ref, kn_ref, vn_ref, *rest,
                       pages_per_step, t_new, page, lam_init):
    del pt_ref
    pp = pages_per_step
    k_refs, v_refs = rest[:pp], rest[pp:2 * pp]
    o_ref, m_sc, l_sc, acc_sc = rest[2 * pp:]
    g = pl.program_id(1)
    rows = 2 * t_new

    @pl.when(g == 0)
    def _():
        m_sc[...] = jnp.full(m_sc.shape, NEG, F32)
        l_sc[...] = jnp.zeros(l_sc.shape, F32)
        acc_sc[...] = jnp.zeros(acc_sc.shape, F32)

    q = q_ref[...]

    def online(h, qs, s_list, v_list):
        m_old = m_sc[h]
        mx = functools.reduce(jnp.maximum, s_list)
        m_new = jnp.maximum(m_old, jnp.max(mx, axis=-1, keepdims=True))
        corr = jnp.exp(m_old - m_new)
        psum = None
        pv = None
        for s, v in zip(s_list, v_list):
            p = jnp.exp(s - m_new)
            psum = p if psum is None else psum + p
            d = _dot(p.astype(BF16), v)
            pv = d if pv is None else pv + d
        m_sc[h] = m_new
        l_sc[h] = l_sc[h] * corr + jnp.sum(psum, axis=-1, keepdims=True)
        acc_sc[h] = acc_sc[h] * corr + pv

    for h in range(DA_HEADS):
        sl = slice(h * DA_V, (h + 1) * DA_V)
        qs = _stack_maps(q[:, sl]).astype(BF16)
        s_list = [_dot_nt(qs, k_refs[u][:, sl].astype(BF16)) for u in range(pp)]
        v_list = [v_refs[u][:, sl].astype(BF16) for u in range(pp)]
        online(h, qs, s_list, v_list)

    @pl.when(g == pl.num_programs(1) - 1)
    def _():
        lam = _lam_value(lamv_ref, lam_init)
        row = lax.broadcasted_iota(jnp.int32, (rows, page), 0)
        col = lax.broadcasted_iota(jnp.int32, (rows, page), 1)
        causal = col <= jnp.where(row >= t_new, row - t_new, row)
        pad = jnp.zeros((page - t_new, DA_V), F32)
        for h in range(DA_HEADS):
            sl = slice(h * DA_V, (h + 1) * DA_V)
            qs = _stack_maps(q[:, sl]).astype(BF16)
            kn = jnp.concatenate([kn_ref[:, sl], pad], axis=0).astype(BF16)
            vn = jnp.concatenate([vn_ref[:, sl], pad], axis=0).astype(BF16)
            s = jnp.where(causal, _dot_nt(qs, kn), NEG)
            online(h, qs, [s], [vn])
            o = acc_sc[h] / l_sc[h]
            od = o[:t_new] - lam * o[t_new:]
            o_ref[:, sl] = _rmsnorm(od, subln_ref[...], SUBLN_EPS) * (1.0 - lam_init)


def _attn_paged(page_table, lamv, subln, q, k_new, v_new, cache_k, cache_v, *,
                pages_per_step, lam_init):
    nb, n_pages = page_table.shape
    n, d_diff = q.shape
    t_new = n // nb
    n_pool, page = cache_k.shape[0], cache_k.shape[1]
    pp = pages_per_step
    ck = cache_k.reshape(n_pool, page, d_diff)
    cv = cache_v.reshape(n_pool, page, d_diff)
    seq_spec = pl.BlockSpec((t_new, d_diff), lambda b, g, pt: (b, 0))

    def page_spec(u):
        return pl.BlockSpec((None, page, d_diff), lambda b, g, pt: (pt[b, g * pp + u], 0, 0))

    const = lambda shape: pl.BlockSpec(shape, lambda b, g, pt: (0,) * len(shape))
    grid_spec = pltpu.PrefetchScalarGridSpec(
        num_scalar_prefetch=1,
        grid=(nb, n_pages // pp),
        in_specs=[const(lamv.shape), const(subln.shape), seq_spec, seq_spec, seq_spec]
        + [page_spec(u) for u in range(pp)] + [page_spec(u) for u in range(pp)],
        out_specs=seq_spec,
        scratch_shapes=[pltpu.VMEM((DA_HEADS, 2 * t_new, LANES), F32),
                        pltpu.VMEM((DA_HEADS, 2 * t_new, LANES), F32),
                        pltpu.VMEM((DA_HEADS, 2 * t_new, DA_V), F32)],
    )
    return pl.pallas_call(
        functools.partial(_attn_paged_kernel, pages_per_step=pp, t_new=t_new, page=page,
                          lam_init=lam_init),
        grid_spec=grid_spec,
        out_shape=jax.ShapeDtypeStruct((n, d_diff), F32),
        compiler_params=_params("arbitrary", "arbitrary"),
        name="attn_paged",
    )(page_table, lamv, subln, q, k_new, v_new, *([ck] * pp), *([cv] * pp))


def _seg_sum(x, ones_bd):
    hi, lo = _split2(x)
    return _dot(hi, ones_bd) + _dot(lo, ones_bd)


def _tri_inverse(a, c):
    row = lax.broadcasted_iota(jnp.int32, (c, c), 0)
    col = lax.broadcasted_iota(jnp.int32, (c, c), 1)
    t = jnp.where(row == col, 1.0, 0.0).astype(F32) + a
    p = a
    n = 1
    while 2 * n < c:
        pb = p.astype(BF16)
        p = _dot(pb, pb)
        t = t + _dot(t.astype(BF16), p.astype(BF16))
        n *= 2
    return t


def _rwkv_kernel(zm_ref, zl_ref, pm_ref, pl_ref, s0_ref, mum_ref, mul_ref, w0_ref, a0_ref,
                 kk_ref, ka_ref, rk_ref, lng_ref, lnb_ref, ww_ref, wa_ref, wg_ref, ones_ref,
                 o_ref, sout_ref, cm_sc, cl_sc, h_sc, *, t_blk, chunk, d_rw):
    c = pl.program_id(1)
    n_pairs = d_rw // RW_PAIR

    @pl.when(c == 0)
    def _():
        cm_sc[...] = pm_ref[...]
        cl_sc[...] = pl_ref[...]
        h_sc[...] = s0_ref[...]

    def shift_mix(z, carry_ref, mu):
        rows = lax.broadcasted_iota(jnp.int32, z.shape, 0)
        zs = jnp.where(rows == 0, carry_ref[...], pltpu.roll(z, 1, axis=0))
        carry_ref[...] = z[t_blk - 1:t_blk, :]
        return z + (zs - z) * mu

    zmix = shift_mix(zm_ref[...], cm_sc, mum_ref[...])
    zlm = shift_mix(zl_ref[...], cl_sc, mul_ref[...])
    r, k, v = zmix[:, :d_rw], zmix[:, d_rw:2 * d_rw], zmix[:, 2 * d_rw:]

    ones_bd = ones_ref[...]
    u = w0_ref[...] + _dot(jnp.tanh(zlm).astype(BF16), ww_ref[...])
    softplus = jnp.maximum(-u, 0.0) + jnp.log(1.0 + jnp.exp(-jnp.abs(u)))
    lw = -jnp.exp(-softplus - 0.5)
    a = _sigmoid(a0_ref[...] + _dot(zlm.astype(BF16), wa_ref[...]))
    gate = _dot(_sigmoid(zlm).astype(BF16), wg_ref[...])
    kk = k * kk_ref[...]
    kk = kk / jnp.maximum(jnp.sqrt(_seg_sum(kk * kk, ones_bd)), 1e-12)
    k2 = k * (1.0 + (a - 1.0) * ka_ref[...])
    av = -kk
    bv = kk * a
    bonus = _seg_sum(r * k2 * rk_ref[...], ones_bd) * v

    if chunk > t_blk:
        zpad = jnp.zeros((chunk - t_blk, d_rw), F32)
        padr = lambda x: jnp.concatenate([x, zpad], axis=0)
        r_c, lw, k2, v_c, av, bv = padr(r), padr(lw), padr(k2), padr(v), padr(av), padr(bv)
    else:
        r_c, v_c = r, v

    row = lax.broadcasted_iota(jnp.int32, (chunk, chunk), 0)
    col = lax.broadcasted_iota(jnp.int32, (chunk, chunk), 1)
    incl = row >= col
    strict = row > col
    tri = jnp.where(incl, 1.0, 0.0).astype(BF16)
    gc = functools.reduce(lambda x, y: x + y, [_dot(tri, part) for part in _split3(lw)])
    gl = gc[chunk - 1:chunk, :]
    w_in_ = jnp.exp(gc)
    w_inv = jnp.exp(-gc)
    w_rem = jnp.exp(gl - gc)
    at = (av * jnp.exp(gc - lw)).astype(BF16)
    rt = (r_c * w_in_).astype(BF16)
    bt = (bv * w_inv).astype(BF16)
    kt = (k2 * w_inv).astype(BF16)
    bh = bv * w_rem
    kh = k2 * w_rem
    vb = v_c.astype(BF16)

    lane = lax.broadcasted_iota(jnp.int32, (1, RW_PAIR), 1)
    first = lane < RW_HEAD
    prow = lax.broadcasted_iota(jnp.int32, (RW_PAIR, RW_PAIR), 0)
    pcol = lax.broadcasted_iota(jnp.int32, (RW_PAIR, RW_PAIR), 1)
    same_head = (prow < RW_HEAD) == (pcol < RW_HEAD)
    row2 = lax.broadcasted_iota(jnp.int32, (2 * chunk, chunk), 0)
    col2 = lax.broadcasted_iota(jnp.int32, (2 * chunk, chunk), 1)
    mask2 = jnp.where(row2 < chunk, row2, row2 - chunk + 1) > col2
    zero_b = jnp.zeros((), BF16)
    upd_rows = -(-2 * chunk // RW_PAIR) * RW_PAIR
    upd_pad = jnp.zeros((upd_rows - 2 * chunk, RW_PAIR), F32)

    ys = []
    for p in range(n_pairs):
        sl = slice(p * RW_PAIR, (p + 1) * RW_PAIR)
        lhs = jnp.concatenate([at[:, sl], rt[:, sl]], axis=0)
        h_old = h_sc[p]
        hh = _dot(lhs, h_old.astype(BF16))
        xk, xb, tinv = [], [], []
        for e in range(2):
            sel = first if e == 0 else jnp.logical_not(first)
            le = jnp.where(sel, lhs, zero_b)
            ab = jnp.where(mask2, _dot_nt(le, bt[:, sl]), 0.0)
            ak = jnp.where(mask2, _dot_nt(le, kt[:, sl]), 0.0)
            xk.append(_dot(ak.astype(BF16), vb[:, sl]))
            xb.append(ab[chunk:].astype(BF16))
            tinv.append(_tri_inverse(ab[:chunk], chunk).astype(BF16))
        xkv = jnp.where(first, xk[0], xk[1])
        rhs = (hh[:chunk] + xkv[:chunk]).astype(BF16)
        uu = jnp.where(first, _dot(tinv[0], rhs), _dot(tinv[1], rhs))
        ub = uu.astype(BF16)
        y = hh[chunk:] + xkv[chunk:] + jnp.where(first, _dot(xb[0], ub), _dot(xb[1], ub))
        ys.append(y[:t_blk])
        pieces_l = [bh[:, sl], kh[:, sl]]
        pieces_r = [uu, v_c[:, sl]]
        if upd_rows > 2 * chunk:
            pieces_l.append(upd_pad)
            pieces_r.append(upd_pad)
        lhs_t = jnp.concatenate(pieces_l, axis=0).T.astype(BF16)
        upd = _dot(lhs_t, jnp.concatenate(pieces_r, axis=0).astype(BF16))
        w_all = jnp.exp(jnp.broadcast_to(gl[:, sl], (RW_PAIR, RW_PAIR))).T
        h_sc[p] = h_old * w_all + jnp.where(same_head, upd, 0.0)

    y = jnp.concatenate(ys, axis=1)
    inv_n = 1.0 / RW_HEAD
    mean = _seg_sum(y, ones_bd) * inv_n
    yc = y - mean
    var = _seg_sum(yc * yc, ones_bd) * inv_n
    o = yc * lax.rsqrt(var + GN_EPS) * lng_ref[...] + lnb_ref[...]
    o_ref[...] = ((o + bonus) * gate).astype(o_ref.dtype)

    @pl.when(c == pl.num_programs(1) - 1)
    def _():
        sout_ref[...] = h_sc[...]


def _rwkv(zm, zl, prev_m, prev_l, s0, prm, *, batch, seq, t_blk, chunk):
    n, w_main = zm.shape
    d_rw = w_main // 3
    n_lora = zl.shape[1]
    n_pairs = d_rw // RW_PAIR
    nc = seq // t_blk
    row = lambda w: pl.BlockSpec((t_blk, w), lambda b, c: (b * nc + c, 0))
    per_b = lambda w: pl.BlockSpec((None, 1, w), lambda b, c: (b, 0, 0))
    st_spec = pl.BlockSpec((None, n_pairs, RW_PAIR, RW_PAIR), lambda b, c: (b, 0, 0, 0))
    vecs = [prm[name] for name in ("mu_m", "mu_l", "w0", "a0", "k_k", "k_a", "r_k", "ln_g", "ln_b")]
    mats = [prm["ww"], prm["wa"], prm["wg"], prm["ones_bd"]]
    return pl.pallas_call(
        functools.partial(_rwkv_kernel, t_blk=t_blk, chunk=chunk, d_rw=d_rw),
        grid=(batch, nc),
        in_specs=[row(w_main), row(n_lora), per_b(w_main), per_b(n_lora), st_spec]
        + [_const_spec(x.shape) for x in vecs + mats],
        out_specs=[row(d_rw), st_spec],
        out_shape=[jax.ShapeDtypeStruct((n, d_rw), F32),
                   jax.ShapeDtypeStruct(s0.shape, F32)],
        scratch_shapes=[pltpu.VMEM((1, w_main), F32), pltpu.VMEM((1, n_lora), F32),
                        pltpu.VMEM((n_pairs, RW_PAIR, RW_PAIR), F32)],
        compiler_params=_params("arbitrary", "arbitrary"),
        name="rwkv_mix",
    )(zm, zl, prev_m.reshape(batch, 1, w_main), prev_l.reshape(batch, 1, n_lora), s0,
      *vecs, *mats)


def _state_to_pairs(s):
    b, h, n, _ = s.shape
    st = jnp.swapaxes(s, -1, -2).reshape(b, h // 2, 2, n, n)
    eye = jnp.eye(2, dtype=s.dtype)
    bd = st[:, :, :, :, None, :] * eye[None, None, :, None, :, None]
    return bd.reshape(b, h // 2, 2 * n, 2 * n)


def _pairs_to_state(hp):
    b, p, n2, _ = hp.shape
    n = n2 // 2
    h6 = hp.reshape(b, p, 2, n, 2, n)
    diag = jnp.stack([h6[:, :, 0, :, 0, :], h6[:, :, 1, :, 1, :]], axis=2)
    return jnp.swapaxes(diag, -1, -2).reshape(b, 2 * p, n, n)


def _ffn_kernel(x_ref, oa_ref, ob_ref, cp_ref, woa_ref, wob_ref, nf_ref, wup_ref, cw_ref,
                cb_ref, wdn_ref, nfin_ref, y_ref, cs_ref, carry_sc, *, d_ff, cw, carried,
                apply_final):
    t = pl.program_id(1)
    tm = x_ref.shape[0]
    groups = tm // SUBLANES

    if carried:
        @pl.when(t == 0)
        def _():
            carry_sc[...] = jnp.zeros(carry_sc.shape, F32)
            carry_sc[:, SUBLANES - (CONV_W - 1):, :] = cp_ref[...]

    x1 = (x_ref[...] + _dot(oa_ref[...].astype(BF16), woa_ref[...])
          + _dot(ob_ref[...].astype(BF16), wob_ref[...]))
    h = _rmsnorm(x1, nf_ref[...], RMS_EPS).astype(BF16)
    t8 = lax.broadcasted_iota(jnp.int32, (groups, SUBLANES, cw), 1)

    def conv(col0):
        cols = slice(col0, col0 + cw)
        up = _dot(h, wup_ref[:, cols]).reshape(groups, SUBLANES, cw)
        if carried:
            prev = carry_sc[:, :, cols]
            if groups > 1:
                prev = jnp.concatenate([prev, up[:-1]], axis=0)
            p6, p7 = prev[:, 6:7, :], prev[:, 7:8, :]
            carry_sc[:, :, cols] = up[groups - 1:]
            cs_ref[:, :, cols] = up[groups - 1:, SUBLANES - (CONV_W - 1):, :]
        else:
            p6, p7 = cp_ref[:, 0:1, cols], cp_ref[:, 1:2, cols]
            cs_ref[:, :, cols] = up[:, SUBLANES - (CONV_W - 1):, :]
        m1 = jnp.where(t8 == 0, p7, pltpu.roll(up, 1, axis=1))
        m2 = jnp.where(t8 == 0, p6, jnp.where(t8 == 1, p7, pltpu.roll(up, 2, axis=1)))
        w = cw_ref[:, cols]
        out = cb_ref[:, cols] + m2 * w[0:1] + m1 * w[1:2] + up * w[2:3]
        return out.reshape(tm, cw)

    acc = jnp.zeros((tm, x_ref.shape[1]), F32)
    for ci in range(d_ff // cw):
        gate = conv(ci * cw)
        val = conv(d_ff + ci * cw)
        act = (gate * _sigmoid(gate) * val).astype(BF16)
        acc = acc + _dot(act, wdn_ref[ci * cw:(ci + 1) * cw, :])
    x2 = x1 + acc
    y_ref[...] = _rmsnorm(x2, nfin_ref[...], RMS_EPS) if apply_final else x2


def _ffn(x, oa, ob, conv_prev, woa, wob, nf, wup, cwt, cb, wdn, nfin, *, batch, seq, tm, cw,
         apply_final):
    n, d = x.shape
    d_mix = oa.shape[1]
    d_ff2 = wup.shape[1]
    d_ff = d_ff2 // 2
    carried = seq >= tm
    if carried:
        nt = seq // tm
        grid = (batch, nt)
        row = lambda w: pl.BlockSpec((tm, w), lambda b, t: (b * nt + t, 0))
        cs_spec = pl.BlockSpec((1, CONV_W - 1, d_ff2), lambda b, t: (b, 0, 0))
        cp_spec = cs_spec
    else:
        seqs = tm // seq
        grid = (batch // seqs, 1)
        row = lambda w: pl.BlockSpec((tm, w), lambda b, t: (b, 0))
        cs_spec = pl.BlockSpec((seqs, CONV_W - 1, d_ff2), lambda b, t: (b, 0, 0))
        cp_spec = cs_spec
    return pl.pallas_call(
        functools.partial(_ffn_kernel, d_ff=d_ff, cw=cw, carried=carried,
                          apply_final=apply_final),
        grid=grid,
        in_specs=[row(d), row(d_mix), row(d_mix), cp_spec, _const_spec(woa.shape),
                  _const_spec(wob.shape), _const_spec(nf.shape), _const_spec(wup.shape),
                  _const_spec(cwt.shape), _const_spec(cb.shape), _const_spec(wdn.shape),
                  _const_spec(nfin.shape)],
        out_specs=[row(d), cs_spec],
        out_shape=[jax.ShapeDtypeStruct((n, d), F32),
                   jax.ShapeDtypeStruct((batch, CONV_W - 1, d_ff2), F32)],
        scratch_shapes=[pltpu.VMEM((1, SUBLANES, d_ff2), F32)],
        compiler_params=_params("arbitrary", "arbitrary"),
        name="out_proj_ffn",
    )(x, oa, ob, conv_prev, woa, wob, nf, wup, cwt, cb, wdn, nfin)


def _pick_tile(n, target):
    t = min(n, target)
    while n % t:
        t //= 2
    return t


def kernel(x_prompt, x_sample, cache_k, cache_v, state_wkv, state_shift, state_conv, page_table, norm_mix, w_in, lam_q1, lam_k1, lam_q2, lam_k2, subln, rw_mu, rw_w0, rw_w_up, rw_a0, rw_a_up, rw_g_up, rw_k_k, rw_k_a, rw_r_k, rw_ln_g, rw_ln_b, w_out, norm_ffn, w_up, conv_w, conv_b, w_down, norm_final):
    depth = w_in.shape[0]
    bp, tp, d = x_prompt.shape
    bs, ts, _ = x_sample.shape
    d_rw = rw_w0.shape[1]
    d_diff = DA_HEADS * DA_V
    n_wl, n_al, n_gl = rw_w_up.shape[1], rw_a_up.shape[1], rw_g_up.shape[1]
    n_lora = n_wl + n_al + n_gl
    n_rw_cols = 3 * d_rw + n_lora

    xp = x_prompt.reshape(bp * tp, d)
    xs = x_sample.reshape(bs * ts, d)
    head_of = jnp.arange(d_rw) // RW_HEAD
    ones_bd = (head_of[:, None] == head_of[None, :]).astype(BF16)
    row2 = lambda vec: vec.reshape(1, -1)

    outs_p = [[] for _ in range(5)]
    outs_s = [[] for _ in range(5)]
    for layer in range(depth):
        lam_init = 0.8 - 0.6 * math.exp(-0.3 * layer)
        last = layer == depth - 1
        wl = w_in[layer]
        wa = wl[:, :3 * d_diff].astype(BF16)
        wr = wl[:, 3 * d_diff:3 * d_diff + 3 * d_rw].astype(BF16)
        wlo = wl[:, 3 * d_diff + 3 * d_rw:].astype(BF16)
        lamv = jnp.stack([lam_q1[layer], lam_k1[layer], lam_q2[layer], lam_k2[layer]])
        sub = row2(subln[layer])
        mu = rw_mu[layer]
        zpad = lambda rows: jnp.zeros((rows, d_rw), BF16)
        prm = {
            "mu_m": row2(mu[:3 * d_rw]), "mu_l": row2(mu[3 * d_rw:]),
            "w0": row2(rw_w0[layer]), "a0": row2(rw_a0[layer]),
            "k_k": row2(rw_k_k[layer]), "k_a": row2(rw_k_a[layer]),
            "r_k": row2(rw_r_k[layer]), "ln_g": row2(rw_ln_g[layer]), "ln_b": row2(rw_ln_b[layer]),
            "ww": jnp.concatenate([rw_w_up[layer].astype(BF16), zpad(n_al + n_gl)], axis=0),
            "wa": jnp.concatenate([zpad(n_wl), rw_a_up[layer].astype(BF16), zpad(n_gl)], axis=0),
            "wg": jnp.concatenate([zpad(n_wl + n_al), rw_g_up[layer].astype(BF16)], axis=0),
            "ones_bd": ones_bd,
        }
        wo = w_out[layer].astype(BF16)
        ffn_w = (wo[:d_diff], wo[d_diff:], row2(norm_ffn[layer]), w_up[layer].astype(BF16),
                 conv_w[layer], row2(conv_b[layer]), w_down[layer].astype(BF16), row2(norm_final))

        def run(x, batch, seq, prev_shift, s0, conv_prev, attn_fn):
            n = batch * seq
            q, k, v, kb, vb, zm, zl = _norm_proj(x, row2(norm_mix[layer]), wa, wr, wlo,
                                                 tm=_pick_tile(n, 256))
            o_a = attn_fn(q, k, v, kb, vb)
            t_blk = min(seq, 64)
            o_b, s_new = _rwkv(zm, zl, prev_shift[:, :3 * d_rw], prev_shift[:, 3 * d_rw:],
                               s0, prm, batch=batch, seq=seq, t_blk=t_blk,
                               chunk=max(t_blk, 16))
            tm = _pick_tile(n, 512 if seq >= 512 else 256)
            y, conv_new = _ffn(x, o_a, o_b, conv_prev, *ffn_w, batch=batch, seq=seq, tm=tm,
                               cw=256, apply_final=last)
            shift_new = jnp.concatenate([zm.reshape(batch, seq, -1)[:, -1],
                                         zl.reshape(batch, seq, -1)[:, -1]], axis=-1)
            return (y, k.reshape(batch, seq, DA_HEADS, DA_V), v.reshape(batch, seq, DA_HEADS, DA_V),
                    _pairs_to_state(s_new), shift_new, conv_new)

        prompt_attn = lambda q, k, v, kb, vb: _attn_prompt(
            lamv, sub, q, kb, vb, batch=bp, seq=tp, tq=_pick_tile(tp, 256), lam_init=lam_init)
        xp, kp, vp, wp, sp, cp = run(
            xp, bp, tp, jnp.zeros((bp, n_rw_cols), F32),
            jnp.zeros((bp, d_rw // RW_PAIR, RW_PAIR, RW_PAIR), F32),
            jnp.zeros((bp, CONV_W - 1, w_up.shape[2]), F32), prompt_attn)
        sample_attn = lambda q, k, v, kb, vb: _attn_paged(
            page_table, lamv, sub, q, k, v, cache_k[layer], cache_v[layer],
            pages_per_step=_pick_tile(page_table.shape[1], 8), lam_init=lam_init)
        xs, ks, vs, ws, ss, cs = run(
            xs, bs, ts, state_shift[layer], _state_to_pairs(state_wkv[layer]),
            state_conv[layer], sample_attn)
        for acc, val in zip(outs_p, (kp, vp, wp, sp, cp)):
            acc.append(val)
        for acc, val in zip(outs_s, (ks, vs, ws, ss, cs)):
            acc.append(val)

    y_prompt = xp.reshape(bp, tp, d)
    y_sample = xs.reshape(bs, ts, d)
    return (y_prompt, y_sample, *[jnp.stack(o) for o in outs_p], *[jnp.stack(o) for o in outs_s])
```

```python
import functools
import math

import jax
import jax.numpy as jnp
from jax import lax
from jax.experimental import pallas as pl
from jax.experimental.pallas import tpu as pltpu

F32 = jnp.float32
BF16 = jnp.bfloat16

LANES = 128
SUBLANES = 8
VMEM_LIMIT_BYTES = 56 * 1024 * 1024

RMS_EPS = 1e-6
SUBLN_EPS = 1e-5
GN_EPS = 64e-5
NEG = -1e30

DA_HEADS = 4
DA_V = 128
DA_QK = 64
RW_HEAD = 64
RW_PAIR = 2 * RW_HEAD
CONV_W = 3


def _params(*sem):
    return pltpu.CompilerParams(dimension_semantics=sem, vmem_limit_bytes=VMEM_LIMIT_BYTES)


def _const_spec(shape):
    zeros = (0,) * len(shape)
    return pl.BlockSpec(shape, lambda *_: zeros, pipeline_mode=pl.Buffered(1))


def _dot(a, b):
    return jnp.dot(a, b, preferred_element_type=F32)


def _dot_nt(a, b):
    return lax.dot_general(a, b, (((1,), (1,)), ((), ())), preferred_element_type=F32)


def _split2(x):
    hi = x.astype(BF16)
    lo = (x - hi.astype(F32)).astype(BF16)
    return hi, lo


def _split3(x):
    hi = x.astype(BF16)
    r1 = x - hi.astype(F32)
    mid = r1.astype(BF16)
    lo = (r1 - mid.astype(F32)).astype(BF16)
    return hi, mid, lo


def _rmsnorm(x, g, eps):
    return x * lax.rsqrt(jnp.mean(x * x, axis=-1, keepdims=True) + eps) * g


def _sigmoid(x):
    return 1.0 / (1.0 + jnp.exp(-x))


def _norm_proj_kernel(x_ref, g_ref, wa_ref, wr_ref, wl_ref,
                      q_ref, k_ref, v_ref, kb_ref, vb_ref, zm_ref, zl_ref, *, d_diff):
    h = _rmsnorm(x_ref[...], g_ref[...], RMS_EPS).astype(BF16)
    pa = _dot(h, wa_ref[...])
    q_ref[...] = pa[:, :d_diff] * (DA_QK ** -0.5)
    k = pa[:, d_diff:2 * d_diff]
    v = pa[:, 2 * d_diff:]
    k_ref[...] = k
    v_ref[...] = v
    kb_ref[...] = k.astype(BF16)
    vb_ref[...] = v.astype(BF16)
    zm_ref[...] = _dot(h, wr_ref[...])
    zl_ref[...] = _dot(h, wl_ref[...])


def _norm_proj(x, g, wa, wr, wl, *, tm):
    n, d = x.shape
    d_diff = wa.shape[1] // 3
    n_rkv, n_lora = wr.shape[1], wl.shape[1]
    row = lambda w: pl.BlockSpec((tm, w), lambda i: (i, 0))
    return pl.pallas_call(
        functools.partial(_norm_proj_kernel, d_diff=d_diff),
        grid=(n // tm,),
        in_specs=[row(d), _const_spec((1, d)), _const_spec(wa.shape), _const_spec(wr.shape),
                  _const_spec(wl.shape)],
        out_specs=[row(d_diff), row(d_diff), row(d_diff), row(d_diff), row(d_diff),
                   row(n_rkv), row(n_lora)],
        out_shape=[jax.ShapeDtypeStruct((n, d_diff), F32),
                   jax.ShapeDtypeStruct((n, d_diff), F32),
                   jax.ShapeDtypeStruct((n, d_diff), F32),
                   jax.ShapeDtypeStruct((n, d_diff), BF16),
                   jax.ShapeDtypeStruct((n, d_diff), BF16),
                   jax.ShapeDtypeStruct((n, n_rkv), F32),
                   jax.ShapeDtypeStruct((n, n_lora), F32)],
        compiler_params=_params("arbitrary"),
        name="norm_proj",
    )(x, g, wa, wr, wl)


def _lam_value(lamv_ref, lam_init):
    lv = lamv_ref[...]
    s1 = jnp.sum(lv[0:1] * lv[1:2], axis=-1, keepdims=True)
    s2 = jnp.sum(lv[2:3] * lv[3:4], axis=-1, keepdims=True)
    return jnp.exp(s1) - jnp.exp(s2) + lam_init


def _stack_maps(q):
    lane = lax.broadcasted_iota(jnp.int32, q.shape, 1)
    zero = jnp.zeros_like(q)
    return jnp.concatenate([jnp.where(lane < DA_QK, q, zero),
                            jnp.where(lane >= DA_QK, q, zero)], axis=0)


def _attn_prompt_kernel(lamv_ref, subln_ref, q_ref, k_ref, v_ref, o_ref, *, tq, lam_init):
    i = pl.program_id(2)
    qs = _stack_maps(q_ref[...]).astype(BF16)

    def update(carry, kj, vj, mask):
        m, l, acc = carry
        s = _dot_nt(qs, kj)
        if mask is not None:
            s = jnp.where(mask, s, NEG)
        m_new = jnp.maximum(m, jnp.max(s, axis=-1, keepdims=True))
        corr = jnp.exp(m - m_new)
        p = jnp.exp(s - m_new)
        l = l * corr + jnp.sum(p, axis=-1, keepdims=True)
        acc = acc * corr + _dot(p.astype(BF16), vj)
        return m_new, l, acc

    def body(j, carry):
        start = pl.multiple_of(j * tq, tq)
        return update(carry, k_ref[pl.ds(start, tq), :], v_ref[pl.ds(start, tq), :], None)

    init = (jnp.full((2 * tq, 1), NEG, F32), jnp.zeros((2 * tq, 1), F32),
            jnp.zeros((2 * tq, DA_V), F32))
    carry = lax.fori_loop(0, i, body, init)
    start = pl.multiple_of(i * tq, tq)
    row = lax.broadcasted_iota(jnp.int32, (2 * tq, tq), 0)
    col = lax.broadcasted_iota(jnp.int32, (2 * tq, tq), 1)
    causal = col <= jnp.where(row >= tq, row - tq, row)
    m, l, acc = update(carry, k_ref[pl.ds(start, tq), :], v_ref[pl.ds(start, tq), :], causal)
    o = acc / l
    od = o[:tq] - _lam_value(lamv_ref, lam_init) * o[tq:]
    o_ref[...] = (_rmsnorm(od, subln_ref[...], SUBLN_EPS) * (1.0 - lam_init)).astype(o_ref.dtype)


def _attn_prompt(lamv, subln, q, kb, vb, *, batch, seq, tq, lam_init):
    n, d_diff = q.shape
    heads = d_diff // DA_V
    nq = seq // tq
    kb3 = kb.reshape(batch, seq, d_diff)
    vb3 = vb.reshape(batch, seq, d_diff)
    qspec = pl.BlockSpec((tq, DA_V), lambda b, h, i: (b * nq + i, h))
    kvspec = pl.BlockSpec((None, seq, DA_V), lambda b, h, i: (b, 0, h))
    return pl.pallas_call(
        functools.partial(_attn_prompt_kernel, tq=tq, lam_init=lam_init),
        grid=(batch, heads, nq),
        in_specs=[_const_spec(lamv.shape), _const_spec(subln.shape), qspec, kvspec, kvspec],
        out_specs=qspec,
        out_shape=jax.ShapeDtypeStruct((n, d_diff), BF16),
        compiler_params=_params("arbitrary", "arbitrary", "arbitrary"),
        name="attn_prompt",
    )(lamv, subln, q, kb3, vb3)


def _attn_paged_kernel(pt_ref, lamv_ref, subln_ref, bias_ref, bias_new_ref, q_ref, kn_ref, vn_ref,
                       *rest, pages_per_step, t_new, lam_init):
    del pt_ref
    pp = pages_per_step
    k_refs, v_refs = rest[:pp], rest[pp:2 * pp]
    o_ref, s_sc, m_sc, l_sc, acc_sc = rest[2 * pp:]
    g = pl.program_id(1)
    n_rows = k_refs[0].shape[0]
    blocks = n_rows // LANES

    @pl.when(g == 0)
    def _():
        m_sc[...] = jnp.full(m_sc.shape, NEG, F32)
        l_sc[...] = jnp.zeros(l_sc.shape, F32)
        acc_sc[...] = jnp.zeros(acc_sc.shape, F32)

    q = q_ref[...]
    q_all = jnp.concatenate(
        [_stack_maps(q[:, h * DA_V:(h + 1) * DA_V]) for h in range(DA_HEADS)], axis=0
    ).astype(BF16)

    def online(n_pg, load_k, load_v, bias):
        for u in range(n_pg):
            s_sc[:, u * n_rows:(u + 1) * n_rows] = _dot_nt(q_all, load_k(u)) + bias
        m_old = m_sc[...]
        mx = s_sc[:, 0:LANES]
        for j in range(1, n_pg * blocks):
            mx = jnp.maximum(mx, s_sc[:, j * LANES:(j + 1) * LANES])
        m_new = jnp.maximum(m_old, jnp.broadcast_to(jnp.max(mx, axis=-1, keepdims=True), mx.shape))
        corr = jnp.exp(m_old - m_new)
        psum = None
        pv = None
        for u in range(n_pg):
            ps = []
            for j in range(u * blocks, (u + 1) * blocks):
                p = jnp.exp(s_sc[:, j * LANES:(j + 1) * LANES] - m_new)
                psum = p if psum is None else psum + p
                ps.append(p.astype(BF16))
            d = _dot(jnp.concatenate(ps, axis=1), load_v(u))
            pv = d if pv is None else pv + d
        m_sc[...] = m_new
        l_sc[...] = l_sc[...] * corr + psum
        acc_sc[...] = acc_sc[...] * corr + pv

    online(pp, lambda u: k_refs[u][...].astype(BF16), lambda u: v_refs[u][...].astype(BF16),
           bias_ref[...])

    @pl.when(g == pl.num_programs(1) - 1)
    def _():
        pad = jnp.zeros((n_rows - kn_ref.shape[0], DA_V), F32)
        kn = jnp.concatenate([kn_ref[...], pad], axis=0).astype(BF16)
        vn = jnp.concatenate([vn_ref[...], pad], axis=0).astype(BF16)
        online(1, lambda u: kn, lambda u: vn, bias_new_ref[...])
        lam = _lam_value(lamv_ref, lam_init)
        o = acc_sc[...] / jnp.sum(l_sc[...], axis=-1, keepdims=True)
        for h in range(DA_HEADS):
            base = 2 * t_new * h
            od = o[base:base + t_new] - lam * o[base + t_new:base + 2 * t_new]
            o_ref[:, h * DA_V:(h + 1) * DA_V] = (
                _rmsnorm(od, subln_ref[...], SUBLN_EPS) * (1.0 - lam_init))


def _attn_paged(page_table, lamv, subln, q, k_new, v_new, cache_k, cache_v, *,
                pages_per_step, pool_offset, lam_init):
    nb, n_pages = page_table.shape
    n, d_diff = q.shape
    t_new = n // nb
    page, heads = cache_k.shape[-3], cache_k.shape[-2]
    n_rows = page * heads
    q_rows = 2 * t_new * heads
    pp = pages_per_step
    ck = cache_k.reshape(-1, n_rows, DA_V)
    cv = cache_v.reshape(-1, n_rows, DA_V)
    kn = k_new.reshape(n * heads, DA_V)
    vn = v_new.reshape(n * heads, DA_V)
    r = jnp.arange(q_rows)[:, None]
    c = jnp.arange(n_rows)[None, :]
    same_head = (c % heads) == (r // (2 * t_new))
    bias = jnp.where(same_head, 0.0, NEG).astype(F32)
    visible = same_head & (c < t_new * heads) & (c // heads <= r % t_new)
    bias_new = jnp.where(visible, 0.0, NEG).astype(F32)
    seq_spec = pl.BlockSpec((t_new, d_diff), lambda b, g, pt: (b, 0))
    new_spec = pl.BlockSpec((t_new * heads, DA_V), lambda b, g, pt: (b, 0))

    def page_spec(u):
        return pl.BlockSpec((None, n_rows, DA_V),
                            lambda b, g, pt: (pool_offset + pt[b, g * pp + u], 0, 0))

    const = lambda shape: pl.BlockSpec(shape, lambda b, g, pt: (0,) * len(shape))
    grid_spec = pltpu.PrefetchScalarGridSpec(
        num_scalar_prefetch=1,
        grid=(nb, n_pages // pp),
        in_specs=[const(lamv.shape), const(subln.shape), const(bias.shape), const(bias_new.shape),
                  seq_spec, new_spec, new_spec]
        + [page_spec(u) for u in range(pp)] + [page_spec(u) for u in range(pp)],
        out_specs=seq_spec,
        scratch_shapes=[pltpu.VMEM((q_rows, pp * n_rows), F32),
                        pltpu.VMEM((q_rows, LANES), F32),
                        pltpu.VMEM((q_rows, LANES), F32),
                        pltpu.VMEM((q_rows, DA_V), F32)],
    )
    return pl.pallas_call(
        functools.partial(_attn_paged_kernel, pages_per_step=pp, t_new=t_new, lam_init=lam_init),
        grid_spec=grid_spec,
        out_shape=jax.ShapeDtypeStruct((n, d_diff), F32),
        compiler_params=_params("arbitrary", "arbitrary"),
        name="attn_paged",
    )(page_table, lamv, subln, bias, bias_new, q, kn, vn, *([ck] * pp), *([cv] * pp))


def _seg_sum(x, ones_pair):
    outs = []
    for p in range(x.shape[1] // RW_PAIR):
        hi, lo = _split2(x[:, p * RW_PAIR:(p + 1) * RW_PAIR])
        outs.append(_dot(hi, ones_pair) + _dot(lo, ones_pair))
    return jnp.concatenate(outs, axis=1)


def _round_robin(tasks):
    results = [None] * len(tasks)
    active = list(enumerate(tasks))
    while active:
        still = []
        for i, task in active:
            try:
                next(task)
                still.append((i, task))
            except StopIteration as stop:
                results[i] = stop.value
        active = still
    return results


def _tri_inverse(mats, c):
    row = lax.broadcasted_iota(jnp.int32, (c, c), 0)
    col = lax.broadcasted_iota(jnp.int32, (c, c), 1)
    eye = jnp.where(row == col, 1.0, 0.0).astype(F32)
    power = list(mats)
    factors = [[(eye + a).astype(BF16)] for a in mats]
    n = 1
    while 2 * n < c:
        yield
        power = [_dot(pb, pb) for pb in (p.astype(BF16) for p in power)]
        for fs, p in zip(factors, power):
            fs.append((eye + p).astype(BF16))
        n *= 2
    while len(factors[0]) > 1:
        yield
        nxt = []
        for fs in factors:
            prod = [_dot(fs[i], fs[i + 1]).astype(BF16) for i in range(0, len(fs) - 1, 2)]
            nxt.append(prod + ([fs[-1]] if len(fs) % 2 else []))
        factors = nxt
    return [fs[0] for fs in factors]


def _rwkv_kernel(zm_ref, zl_ref, pm_ref, pl_ref, s0_ref, mum_ref, mul_ref, w0_ref, a0_ref,
                 kk_ref, ka_ref, rk_ref, lng_ref, lnb_ref, ww_ref, wa_ref, wg_ref, ones_ref,
                 o_ref, sout_ref, cm_sc, cl_sc, h_sc, *, g_blk, t_blk, chunk, d_rw):
    c = pl.program_id(1)
    n_pairs = d_rw // RW_PAIR

    @pl.when(c == 0)
    def _():
        cm_sc[...] = pm_ref[...]
        cl_sc[...] = pl_ref[...]
        h_sc[...] = s0_ref[...]

    def shift_mix(z_ref, carry_ref, mu):
        out = []
        for g in range(g_blk):
            z = z_ref[g]
            rows = lax.broadcasted_iota(jnp.int32, z.shape, 0)
            zs = jnp.where(rows == 0, carry_ref[g], pltpu.roll(z, 1, axis=0))
            carry_ref[g] = z[t_blk - 1:t_blk, :]
            out.append(z + (zs - z) * mu)
        return jnp.concatenate(out, axis=0) if g_blk > 1 else out[0]

    zmix = shift_mix(zm_ref, cm_sc, mum_ref[...])
    zlm = shift_mix(zl_ref, cl_sc, mul_ref[...])
    r, k, v = zmix[:, :d_rw], zmix[:, d_rw:2 * d_rw], zmix[:, 2 * d_rw:]

    ones_bd = ones_ref[...]
    u = w0_ref[...] + _dot(jnp.tanh(zlm).astype(BF16), ww_ref[...])
    softplus = jnp.maximum(-u, 0.0) + jnp.log(1.0 + jnp.exp(-jnp.abs(u)))
    lw = -jnp.exp(-softplus - 0.5)
    a = _sigmoid(a0_ref[...] + _dot(zlm.astype(BF16), wa_ref[...]))
    gate = _dot(_sigmoid(zlm).astype(BF16), wg_ref[...])
    kk = k * kk_ref[...]
    kk = kk / jnp.maximum(jnp.sqrt(_seg_sum(kk * kk, ones_bd)), 1e-12)
    k2 = k * (1.0 + (a - 1.0) * ka_ref[...])
    av = -kk
    bv = kk * a
    bonus = _seg_sum(r * k2 * rk_ref[...], ones_bd) * v

    row = lax.broadcasted_iota(jnp.int32, (chunk, chunk), 0)
    col = lax.broadcasted_iota(jnp.int32, (chunk, chunk), 1)
    tri = jnp.where(row >= col, 1.0, 0.0).astype(BF16)
    zpad = jnp.zeros((chunk - t_blk, d_rw), F32) if chunk > t_blk else None

    def seq_rows(x, g):
        xg = x[g * t_blk:(g + 1) * t_blk]
        return xg if zpad is None else jnp.concatenate([xg, zpad], axis=0)

    lane = lax.broadcasted_iota(jnp.int32, (1, RW_PAIR), 1)
    first = lane < RW_HEAD
    prow = lax.broadcasted_iota(jnp.int32, (RW_PAIR, RW_PAIR), 0)
    pcol = lax.broadcasted_iota(jnp.int32, (RW_PAIR, RW_PAIR), 1)
    same_head = (prow < RW_HEAD) == (pcol < RW_HEAD)
    row2 = lax.broadcasted_iota(jnp.int32, (2 * chunk, chunk), 0)
    col2 = lax.broadcasted_iota(jnp.int32, (2 * chunk, chunk), 1)
    mask2 = jnp.where(row2 < chunk, row2, row2 - chunk + 1) > col2
    zero_b = jnp.zeros((), BF16)
    upd_rows = -(-2 * chunk // RW_PAIR) * RW_PAIR
    upd_pad = jnp.zeros((upd_rows - 2 * chunk, RW_PAIR), F32)

    def chunk_pair(g, p, at, rt, bt, kt, bh, kh, v_c, vb, gl):
        sl = slice(p * RW_PAIR, (p + 1) * RW_PAIR)
        lhs = jnp.concatenate([at[:, sl], rt[:, sl]], axis=0)
        h_old = h_sc[g, p]
        hh = _dot(lhs, h_old.astype(BF16))
        ab, ak = [], []
        for e in range(2):
            sel = first if e == 0 else jnp.logical_not(first)
            le = jnp.where(sel, lhs, zero_b)
            ab.append(jnp.where(mask2, _dot_nt(le, bt[:, sl]), 0.0))
            ak.append(jnp.where(mask2, _dot_nt(le, kt[:, sl]), 0.0))
        yield
        xk = [_dot(x.astype(BF16), vb[:, sl]) for x in ak]
        xb = [x[chunk:].astype(BF16) for x in ab]
        tinv = yield from _tri_inverse([x[:chunk] for x in ab], chunk)
        xkv = jnp.where(first, xk[0], xk[1])
        rhs = (hh[:chunk] + xkv[:chunk]).astype(BF16)
        yield
        uu = jnp.where(first, _dot(tinv[0], rhs), _dot(tinv[1], rhs))
        ub = uu.astype(BF16)
        yield
        y = hh[chunk:] + xkv[chunk:] + jnp.where(first, _dot(xb[0], ub), _dot(xb[1], ub))
        pieces_l = [bh[:, sl], kh[:, sl]]
        pieces_r = [uu, v_c[:, sl]]
        if upd_rows > 2 * chunk:
            pieces_l.append(upd_pad)
            pieces_r.append(upd_pad)
        lhs_t = jnp.concatenate(pieces_l, axis=0).T.astype(BF16)
        upd = _dot(lhs_t, jnp.concatenate(pieces_r, axis=0).astype(BF16))
        w_all = jnp.exp(jnp.broadcast_to(gl[:, sl], (RW_PAIR, RW_PAIR))).T
        h_sc[g, p] = h_old * w_all + jnp.where(same_head, upd, 0.0)
        return y[:t_blk]

    tasks = []
    for g in range(g_blk):
        lw_g = seq_rows(lw, g)
        gc = functools.reduce(lambda x, y: x + y, [_dot(tri, part) for part in _split3(lw_g)])
        gl = gc[chunk - 1:chunk, :]
        w_inv = jnp.exp(-gc)
        w_rem = jnp.exp(gl - gc)
        r_g, k_g, v_g, a_g, b_g = (seq_rows(x, g) for x in (r, k2, v, av, bv))
        at = (a_g * jnp.exp(gc - lw_g)).astype(BF16)
        rt = (r_g * jnp.exp(gc)).astype(BF16)
        bt = (b_g * w_inv).astype(BF16)
        kt = (k_g * w_inv).astype(BF16)
        bh = b_g * w_rem
        kh = k_g * w_rem
        vb = v_g.astype(BF16)
        tasks += [chunk_pair(g, p, at, rt, bt, kt, bh, kh, v_g, vb, gl) for p in range(n_pairs)]
    ys = _round_robin(tasks)
    y_rows = [jnp.concatenate(ys[g * n_pairs:(g + 1) * n_pairs], axis=1) for g in range(g_blk)]

    y = jnp.concatenate(y_rows, axis=0) if g_blk > 1 else y_rows[0]
    inv_n = 1.0 / RW_HEAD
    mean = _seg_sum(y, ones_bd) * inv_n
    yc = y - mean
    var = _seg_sum(yc * yc, ones_bd) * inv_n
    o = yc * lax.rsqrt(var + GN_EPS) * lng_ref[...] + lnb_ref[...]
    out = ((o + bonus) * gate).astype(o_ref.dtype)
    for g in range(g_blk):
        o_ref[g] = out[g * t_blk:(g + 1) * t_blk]

    @pl.when(c == pl.num_programs(1) - 1)
    def _():
        sout_ref[...] = h_sc[...]


def _rwkv(zm, zl, prev_m, prev_l, s0, prm, *, batch, seq, g_blk, t_blk, chunk):
    n, w_main = zm.shape
    d_rw = w_main // 3
    n_lora = zl.shape[1]
    n_pairs = d_rw // RW_PAIR
    row = lambda w: pl.BlockSpec((g_blk, t_blk, w), lambda b, c: (b, c, 0))
    per_b = lambda w: pl.BlockSpec((g_blk, 1, w), lambda b, c: (b, 0, 0))
    st_spec = pl.BlockSpec((g_blk, n_pairs, RW_PAIR, RW_PAIR), lambda b, c: (b, 0, 0, 0))
    vecs = [prm[name] for name in ("mu_m", "mu_l", "w0", "a0", "k_k", "k_a", "r_k", "ln_g", "ln_b")]
    mats = [prm["ww"], prm["wa"], prm["wg"], prm["ones_pair"]]
    o_b, s_new = pl.pallas_call(
        functools.partial(_rwkv_kernel, g_blk=g_blk, t_blk=t_blk, chunk=chunk, d_rw=d_rw),
        grid=(batch // g_blk, seq // t_blk),
        in_specs=[row(w_main), row(n_lora), per_b(w_main), per_b(n_lora), st_spec]
        + [_const_spec(x.shape) for x in vecs + mats],
        out_specs=[row(d_rw), st_spec],
        out_shape=[jax.ShapeDtypeStruct((batch, seq, d_rw), F32),
                   jax.ShapeDtypeStruct(s0.shape, F32)],
        scratch_shapes=[pltpu.VMEM((g_blk, 1, w_main), F32), pltpu.VMEM((g_blk, 1, n_lora), F32),
                        pltpu.VMEM((g_blk, n_pairs, RW_PAIR, RW_PAIR), F32)],
        compiler_params=_params("arbitrary", "arbitrary"),
        name="rwkv_mix",
    )(zm.reshape(batch, seq, w_main), zl.reshape(batch, seq, n_lora),
      prev_m.reshape(batch, 1, w_main), prev_l.reshape(batch, 1, n_lora), s0, *vecs, *mats)
    return o_b.reshape(n, d_rw), s_new


def _state_to_pairs(s):
    b, h, n, _ = s.shape
    st = jnp.swapaxes(s, -1, -2).reshape(b, h // 2, 2, n, n)
    eye = jnp.eye(2, dtype=s.dtype)
    bd = st[:, :, :, :, None, :] * eye[None, None, :, None, :, None]
    return bd.reshape(b, h // 2, 2 * n, 2 * n)


def _pairs_to_state(hp):
    b, p, n2, _ = hp.shape
    n = n2 // 2
    h6 = hp.reshape(b, p, 2, n, 2, n)
    diag = jnp.stack([h6[:, :, 0, :, 0, :], h6[:, :, 1, :, 1, :]], axis=2)
    return jnp.swapaxes(diag, -1, -2).reshape(b, 2 * p, n, n)


def _ffn_kernel(x_ref, oa_ref, ob_ref, cp_ref, woa_ref, wob_ref, nf_ref, wup_ref, cw_ref,
                cb_ref, wdn_ref, nfin_ref, y_ref, cs_ref, carry_sc, *, d_ff, cw, carried,
                apply_final):
    t = pl.program_id(1)
    tm = x_ref.shape[0]
    groups = tm // SUBLANES

    if carried:
        @pl.when(t == 0)
        def _():
            carry_sc[...] = jnp.zeros(carry_sc.shape, F32)
            carry_sc[:, SUBLANES - (CONV_W - 1):, :] = cp_ref[...]

    x1 = (x_ref[...] + _dot(oa_ref[...].astype(BF16), woa_ref[...])
          + _dot(ob_ref[...].astype(BF16), wob_ref[...]))
    h = _rmsnorm(x1, nf_ref[...], RMS_EPS).astype(BF16)
    t8 = lax.broadcasted_iota(jnp.int32, (groups, SUBLANES, cw), 1)

    def up_proj(ci):
        return [_dot(h, wup_ref[:, col0:col0 + cw]) for col0 in (ci * cw, d_ff + ci * cw)]

    def conv(up, col0):
        cols = slice(col0, col0 + cw)
        up = up.reshape(groups, SUBLANES, cw)
        if carried:
            prev = carry_sc[:, :, cols]
            if groups > 1:
                prev = jnp.concatenate([prev, up[:-1]], axis=0)
            p6, p7 = prev[:, 6:7, :], prev[:, 7:8, :]
            carry_sc[:, :, cols] = up[groups - 1:]
            cs_ref[:, :, cols] = up[groups - 1:, SUBLANES - (CONV_W - 1):, :]
        else:
            p6, p7 = cp_ref[:, 0:1, cols], cp_ref[:, 1:2, cols]
            cs_ref[:, :, cols] = up[:, SUBLANES - (CONV_W - 1):, :]
        m1 = jnp.where(t8 == 0, p7, pltpu.roll(up, 1, axis=1))
        m2 = jnp.where(t8 == 0, p6, jnp.where(t8 == 1, p7, pltpu.roll(up, 2, axis=1)))
        w = cw_ref[:, cols]
        out = cb_ref[:, cols] + m2 * w[0:1] + m1 * w[1:2] + up * w[2:3]
        return out.reshape(tm, cw)

    n_chunks = d_ff // cw
    acc = jnp.zeros((tm, x_ref.shape[1]), F32)
    ups = up_proj(0)
    for ci in range(n_chunks):
        nxt = up_proj(ci + 1) if ci + 1 < n_chunks else None
        gate = conv(ups[0], ci * cw)
        val = conv(ups[1], d_ff + ci * cw)
        act = (gate * _sigmoid(gate) * val).astype(BF16)
        acc = acc + _dot(act, wdn_ref[ci * cw:(ci + 1) * cw, :])
        ups = nxt
    x2 = x1 + acc
    y_ref[...] = _rmsnorm(x2, nfin_ref[...], RMS_EPS) if apply_final else x2


def _ffn(x, oa, ob, conv_prev, woa, wob, nf, wup, cwt, cb, wdn, nfin, *, batch, seq, tm, cw,
         apply_final):
    n, d = x.shape
    d_mix = oa.shape[1]
    d_ff2 = wup.shape[1]
    d_ff = d_ff2 // 2
    carried = seq >= tm
    if carried:
        nt = seq // tm
        grid = (batch, nt)
        row = lambda w: pl.BlockSpec((tm, w), lambda b, t: (b * nt + t, 0))
        cs_spec = pl.BlockSpec((1, CONV_W - 1, d_ff2), lambda b, t: (b, 0, 0))
        cp_spec = cs_spec
    else:
        seqs = tm // seq
        grid = (batch // seqs, 1)
        row = lambda w: pl.BlockSpec((tm, w), lambda b, t: (b, 0))
        cs_spec = pl.BlockSpec((seqs, CONV_W - 1, d_ff2), lambda b, t: (b, 0, 0))
        cp_spec = cs_spec
    return pl.pallas_call(
        functools.partial(_ffn_kernel, d_ff=d_ff, cw=cw, carried=carried,
                          apply_final=apply_final),
        grid=grid,
        in_specs=[row(d), row(d_mix), row(d_mix), cp_spec, _const_spec(woa.shape),
                  _const_spec(wob.shape), _const_spec(nf.shape), _const_spec(wup.shape),
                  _const_spec(cwt.shape), _const_spec(cb.shape), _const_spec(wdn.shape),
                  _const_spec(nfin.shape)],
        out_specs=[row(d), cs_spec],
        out_shape=[jax.ShapeDtypeStruct((n, d), F32),
                   jax.ShapeDtypeStruct((batch, CONV_W - 1, d_ff2), F32)],
        scratch_shapes=[pltpu.VMEM((1, SUBLANES, d_ff2), F32)],
        compiler_params=_params("arbitrary", "arbitrary"),
        name="out_proj_ffn",
    )(x, oa, ob, conv_prev, woa, wob, nf, wup, cwt, cb, wdn, nfin)


def _pick_tile(n, target):
    t = min(n, target)
    while n % t:
        t //= 2
    return t


def kernel(x_prompt, x_sample, cache_k, cache_v, state_wkv, state_shift, state_conv, page_table, norm_mix, w_in, lam_q1, lam_k1, lam_q2, lam_k2, subln, rw_mu, rw_w0, rw_w_up, rw_a0, rw_a_up, rw_g_up, rw_k_k, rw_k_a, rw_r_k, rw_ln_g, rw_ln_b, w_out, norm_ffn, w_up, conv_w, conv_b, w_down, norm_final):
    depth = w_in.shape[0]
    bp, tp, d = x_prompt.shape
    bs, ts, _ = x_sample.shape
    d_rw = rw_w0.shape[1]
    d_diff = DA_HEADS * DA_V
    n_wl, n_al, n_gl = rw_w_up.shape[1], rw_a_up.shape[1], rw_g_up.shape[1]
    n_lora = n_wl + n_al + n_gl
    n_rw_cols = 3 * d_rw + n_lora

    xp = x_prompt.reshape(bp * tp, d)
    xs = x_sample.reshape(bs * ts, d)
    head_of = jnp.arange(RW_PAIR) // RW_HEAD
    ones_pair = (head_of[:, None] == head_of[None, :]).astype(BF16)
    row2 = lambda vec: vec.reshape(1, -1)

    outs_p = [[] for _ in range(5)]
    outs_s = [[] for _ in range(5)]
    for layer in range(depth):
        lam_init = 0.8 - 0.6 * math.exp(-0.3 * layer)
        last = layer == depth - 1
        wl = w_in[layer]
        wa = wl[:, :3 * d_diff].astype(BF16)
        wr = wl[:, 3 * d_diff:3 * d_diff + 3 * d_rw].astype(BF16)
        wlo = wl[:, 3 * d_diff + 3 * d_rw:].astype(BF16)
        lamv = jnp.stack([lam_q1[layer], lam_k1[layer], lam_q2[layer], lam_k2[layer]])
        sub = row2(subln[layer])
        mu = rw_mu[layer]
        zpad = lambda rows: jnp.zeros((rows, d_rw), BF16)
        prm = {
            "mu_m": row2(mu[:3 * d_rw]), "mu_l": row2(mu[3 * d_rw:]),
            "w0": row2(rw_w0[layer]), "a0": row2(rw_a0[layer]),
            "k_k": row2(rw_k_k[layer]), "k_a": row2(rw_k_a[layer]),
            "r_k": row2(rw_r_k[layer]), "ln_g": row2(rw_ln_g[layer]), "ln_b": row2(rw_ln_b[layer]),
            "ww": jnp.concatenate([rw_w_up[layer].astype(BF16), zpad(n_al + n_gl)], axis=0),
            "wa": jnp.concatenate([zpad(n_wl), rw_a_up[layer].astype(BF16), zpad(n_gl)], axis=0),
            "wg": jnp.concatenate([zpad(n_wl + n_al), rw_g_up[layer].astype(BF16)], axis=0),
            "ones_pair": ones_pair,
        }
        wo = w_out[layer].astype(BF16)
        ffn_w = (wo[:d_diff], wo[d_diff:], row2(norm_ffn[layer]), w_up[layer].astype(BF16),
                 conv_w[layer], row2(conv_b[layer]), w_down[layer].astype(BF16), row2(norm_final))

        def run(x, batch, seq, prev_shift, s0, conv_prev, attn_fn):
            n = batch * seq
            q, k, v, kb, vb, zm, zl = _norm_proj(x, row2(norm_mix[layer]), wa, wr, wlo,
                                                 tm=_pick_tile(n, 256))
            o_a = attn_fn(q, k, v, kb, vb)
            t_blk = min(seq, 64)
            o_b, s_new = _rwkv(zm, zl, prev_shift[:, :3 * d_rw], prev_shift[:, 3 * d_rw:],
                               s0, prm, batch=batch, seq=seq,
                               g_blk=_pick_tile(batch, 4 if t_blk == 64 else 8),
                               t_blk=t_blk, chunk=max(t_blk, 16))
            tm = _pick_tile(n, 512 if seq >= 512 else 256)
            y, conv_new = _ffn(x, o_a, o_b, conv_prev, *ffn_w, batch=batch, seq=seq, tm=tm,
                               cw=256, apply_final=last)
            shift_new = jnp.concatenate([zm.reshape(batch, seq, -1)[:, -1],
                                         zl.reshape(batch, seq, -1)[:, -1]], axis=-1)
            return (y, k.reshape(batch, seq, DA_HEADS, DA_V), v.reshape(batch, seq, DA_HEADS, DA_V),
                    _pairs_to_state(s_new), shift_new, conv_new)

        prompt_attn = lambda q, k, v, kb, vb: _attn_prompt(
            lamv, sub, q, kb, vb, batch=bp, seq=tp, tq=_pick_tile(tp, 256), lam_init=lam_init)
        xp, kp, vp, wp, sp, cp = run(
            xp, bp, tp, jnp.zeros((bp, n_rw_cols), F32),
            jnp.zeros((bp, d_rw // RW_PAIR, RW_PAIR, RW_PAIR), F32),
            jnp.zeros((bp, CONV_W - 1, w_up.shape[2]), F32), prompt_attn)
        sample_attn = lambda q, k, v, kb, vb: _attn_paged(
            page_table, lamv, sub, q, k, v, cache_k, cache_v,
            pages_per_step=_pick_tile(page_table.shape[1], 8),
            pool_offset=layer * cache_k.shape[1], lam_init=lam_init)
        xs, ks, vs, ws, ss, cs = run(
            xs, bs, ts, state_shift[layer], _state_to_pairs(state_wkv[layer]),
            state_conv[layer], sample_attn)
        for acc, val in zip(outs_p, (kp, vp, wp, sp, cp)):
            acc.append(val)
        for acc, val in zip(outs_s, (ks, vs, ws, ss, cs)):
            acc.append(val)

    y_prompt = xp.reshape(bp, tp, d)
    y_sample = xs.reshape(bs, ts, d)
    return (y_prompt, y_sample, *[jnp.stack(o) for o in outs_p], *[jnp.stack(o) for o in outs_s])
```

```python
import functools
import math

import jax
import jax.numpy as jnp
from jax import lax
from jax.experimental import pallas as pl
from jax.experimental.pallas import tpu as pltpu

F32 = jnp.float32
BF16 = jnp.bfloat16

LANES = 128
SUBLANES = 8
VMEM_LIMIT_BYTES = 56 * 1024 * 1024

RMS_EPS = 1e-6
SUBLN_EPS = 1e-5
GN_EPS = 64e-5
NEG = -1e30

DA_HEADS = 4
DA_V = 128
DA_QK = 64
RW_HEAD = 64
RW_PAIR = 2 * RW_HEAD
CONV_W = 3


def _params(*sem):
    return pltpu.CompilerParams(dimension_semantics=sem, vmem_limit_bytes=VMEM_LIMIT_BYTES)


def _const_spec(shape):
    zeros = (0,) * len(shape)
    return pl.BlockSpec(shape, lambda *_: zeros, pipeline_mode=pl.Buffered(1))


def _dot(a, b):
    return jnp.dot(a, b, preferred_element_type=F32)


def _dot_nt(a, b):
    return lax.dot_general(a, b, (((1,), (1,)), ((), ())), preferred_element_type=F32)


def _split2(x):
    hi = x.astype(BF16)
    lo = (x - hi.astype(F32)).astype(BF16)
    return hi, lo


def _split3(x):
    hi = x.astype(BF16)
    r1 = x - hi.astype(F32)
    mid = r1.astype(BF16)
    lo = (r1 - mid.astype(F32)).astype(BF16)
    return hi, mid, lo


def _rmsnorm(x, g, eps):
    return x * lax.rsqrt(jnp.mean(x * x, axis=-1, keepdims=True) + eps) * g


def _sigmoid(x):
    return 1.0 / (1.0 + jnp.exp(-x))


def _norm_proj_kernel(x_ref, g_ref, wa_ref, wr_ref, wl_ref,
                      q_ref, k_ref, v_ref, kb_ref, vb_ref, zm_ref, zl_ref, *, d_diff):
    h = _rmsnorm(x_ref[...], g_ref[...], RMS_EPS).astype(BF16)
    pa = _dot(h, wa_ref[...])
    q_ref[...] = pa[:, :d_diff] * (DA_QK ** -0.5)
    k = pa[:, d_diff:2 * d_diff]
    v = pa[:, 2 * d_diff:]
    k_ref[...] = k
    v_ref[...] = v
    kb_ref[...] = k.astype(BF16)
    vb_ref[...] = v.astype(BF16)
    zm_ref[...] = _dot(h, wr_ref[...])
    zl_ref[...] = _dot(h, wl_ref[...])


def _norm_proj(x, g, wa, wr, wl, *, tm):
    n, d = x.shape
    d_diff = wa.shape[1] // 3
    n_rkv, n_lora = wr.shape[1], wl.shape[1]
    row = lambda w: pl.BlockSpec((tm, w), lambda i: (i, 0))
    return pl.pallas_call(
        functools.partial(_norm_proj_kernel, d_diff=d_diff),
        grid=(n // tm,),
        in_specs=[row(d), _const_spec((1, d)), _const_spec(wa.shape), _const_spec(wr.shape),
                  _const_spec(wl.shape)],
        out_specs=[row(d_diff), row(d_diff), row(d_diff), row(d_diff), row(d_diff),
                   row(n_rkv), row(n_lora)],
        out_shape=[jax.ShapeDtypeStruct((n, d_diff), F32),
                   jax.ShapeDtypeStruct((n, d_diff), F32),
                   jax.ShapeDtypeStruct((n, d_diff), F32),
                   jax.ShapeDtypeStruct((n, d_diff), BF16),
                   jax.ShapeDtypeStruct((n, d_diff), BF16),
                   jax.ShapeDtypeStruct((n, n_rkv), F32),
                   jax.ShapeDtypeStruct((n, n_lora), F32)],
        compiler_params=_params("arbitrary"),
        name="norm_proj",
    )(x, g, wa, wr, wl)


def _lam_value(lamv_ref, lam_init):
    lv = lamv_ref[...]
    s1 = jnp.sum(lv[0:1] * lv[1:2], axis=-1, keepdims=True)
    s2 = jnp.sum(lv[2:3] * lv[3:4], axis=-1, keepdims=True)
    return jnp.exp(s1) - jnp.exp(s2) + lam_init


def _stack_maps(q):
    lane = lax.broadcasted_iota(jnp.int32, q.shape, 1)
    zero = jnp.zeros_like(q)
    return jnp.concatenate([jnp.where(lane < DA_QK, q, zero),
                            jnp.where(lane >= DA_QK, q, zero)], axis=0)


def _attn_prompt_kernel(lamv_ref, subln_ref, bias_ref, q_ref, k_ref, v_ref, o_ref,
                        m_sc, l_sc, acc_sc, *, tq, heads, lam_init):
    i = pl.program_id(1)
    q = q_ref[...]
    lanes = [slice(h * DA_V, (h + 1) * DA_V) for h in range(heads)]
    qs = [_stack_maps(q[:, sl]).astype(BF16) for sl in lanes]
    m_sc[...] = jnp.full(m_sc.shape, NEG, F32)
    l_sc[...] = jnp.zeros(l_sc.shape, F32)
    acc_sc[...] = jnp.zeros(acc_sc.shape, F32)

    def update(start, width, bias):
        rows = pl.ds(start, width)
        scores = [_dot_nt(qs[h], k_ref[rows, lanes[h]]) for h in range(heads)]
        for h in range(heads):
            s = scores[h] if bias is None else scores[h] + bias
            blocks = [s[:, c:c + LANES] for c in range(0, width, LANES)]
            m = m_sc[h]
            row_max = jnp.max(functools.reduce(jnp.maximum, blocks), axis=-1, keepdims=True)
            m_new = jnp.maximum(m, jnp.broadcast_to(row_max, m.shape))
            corr = jnp.exp(m - m_new)
            ps = [jnp.exp(blk - m_new) for blk in blocks]
            m_sc[h] = m_new
            l_sc[h] = l_sc[h] * corr + functools.reduce(lambda a, b: a + b, ps)
            p = jnp.concatenate([x.astype(BF16) for x in ps], axis=1)
            acc_sc[h] = acc_sc[h] * corr + _dot(p, v_ref[rows, lanes[h]])

    @pl.loop(0, i // 2)
    def _(j):
        update(pl.multiple_of(j * 2 * tq, 2 * tq), 2 * tq, None)

    @pl.when(i % 2 == 1)
    def _():
        update(pl.multiple_of((i - 1) * tq, tq), tq, None)

    update(pl.multiple_of(i * tq, tq), tq, bias_ref[...])
    lam = _lam_value(lamv_ref, lam_init)
    for h in range(heads):
        o = acc_sc[h] / jnp.sum(l_sc[h], axis=-1, keepdims=True)
        od = o[:tq] - lam * o[tq:]
        o_ref[:, lanes[h]] = (_rmsnorm(od, subln_ref[...], SUBLN_EPS)
                              * (1.0 - lam_init)).astype(o_ref.dtype)


def _attn_prompt(lamv, subln, q, kb, vb, *, batch, seq, tq, lam_init):
    n, d_diff = q.shape
    heads = d_diff // DA_V
    nq = seq // tq
    kb3 = kb.reshape(batch, seq, d_diff)
    vb3 = vb.reshape(batch, seq, d_diff)
    r = jnp.arange(2 * tq)[:, None] % tq
    causal_bias = jnp.where(jnp.arange(tq)[None, :] <= r, 0.0, NEG).astype(F32)
    qspec = pl.BlockSpec((tq, d_diff), lambda b, i: (b * nq + i, 0))
    kvspec = pl.BlockSpec((None, seq, d_diff), lambda b, i: (b, 0, 0))
    return pl.pallas_call(
        functools.partial(_attn_prompt_kernel, tq=tq, heads=heads, lam_init=lam_init),
        grid=(batch, nq),
        in_specs=[_const_spec(lamv.shape), _const_spec(subln.shape),
                  _const_spec(causal_bias.shape), qspec, kvspec, kvspec],
        out_specs=qspec,
        out_shape=jax.ShapeDtypeStruct((n, d_diff), BF16),
        scratch_shapes=[pltpu.VMEM((heads, 2 * tq, LANES), F32),
                        pltpu.VMEM((heads, 2 * tq, LANES), F32),
                        pltpu.VMEM((heads, 2 * tq, DA_V), F32)],
        compiler_params=_params("arbitrary", "arbitrary"),
        name="attn_prompt",
    )(lamv, subln, causal_bias, q, kb3, vb3)


def _attn_paged_kernel(pt_ref, lamv_ref, subln_ref, bias_ref, bias_new_ref, q_ref, kn_ref, vn_ref,
                       *rest, pages_per_step, t_new, lam_init):
    del pt_ref
    pp = pages_per_step
    k_refs, v_refs = rest[:pp], rest[pp:2 * pp]
    o_ref, s_sc, m_sc, l_sc, acc_sc = rest[2 * pp:]
    g = pl.program_id(1)
    half = k_refs[0].shape[0] // 2
    q_rows = 2 * t_new * DA_HEADS
    group = 4

    @pl.when(g == 0)
    def _():
        m_sc[...] = jnp.full(m_sc.shape, NEG, F32)
        l_sc[...] = jnp.zeros(l_sc.shape, F32)
        acc_sc[...] = jnp.zeros(acc_sc.shape, F32)

    q = q_ref[...]
    q_all = jnp.concatenate(
        [_stack_maps(q[:, h * DA_V:(h + 1) * DA_V]) for h in range(DA_HEADS)], axis=0)
    zq = jnp.zeros_like(q_all)
    wq = jnp.concatenate([jnp.concatenate([q_all, zq], axis=1),
                          jnp.concatenate([zq, q_all], axis=1)], axis=0).astype(BF16)

    def wide(x):
        return jnp.concatenate([x[:half], x[half:]], axis=1).astype(BF16)

    def to_rows(vec):
        return jnp.broadcast_to(vec, (LANES, LANES)).T[:q_rows]

    def online(n_pg, load_k, load_v, bias):
        for u in range(n_pg):
            s_sc[:, u * LANES:(u + 1) * LANES] = _dot_nt(load_k(u), wq) + bias
        for u0 in range(0, n_pg, group):
            pages = range(u0, min(u0 + group, n_pg))
            m_old = m_sc[...]
            mx = functools.reduce(jnp.maximum, [s_sc[:, u * LANES:(u + 1) * LANES] for u in pages])
            cm = jnp.max(mx, axis=0, keepdims=True)
            m_new = jnp.maximum(m_old, jnp.maximum(cm, pltpu.roll(cm, q_rows, axis=1)))
            corr = jnp.exp(m_old - m_new)
            psum = None
            pv = None
            for u in pages:
                p = jnp.exp(s_sc[:, u * LANES:(u + 1) * LANES] - m_new)
                psum = p if psum is None else psum + p
                d = _dot(p.T.astype(BF16), load_v(u))
                pv = d if pv is None else pv + d
            m_sc[...] = m_new
            l_sc[...] = l_sc[...] * corr + jnp.sum(psum, axis=0, keepdims=True)
            acc_sc[...] = acc_sc[...] * to_rows(corr) + pv[:q_rows, :DA_V] + pv[q_rows:, DA_V:]

    online(pp, lambda u: wide(k_refs[u][...]), lambda u: wide(v_refs[u][...]), bias_ref[...])

    @pl.when(g == pl.num_programs(1) - 1)
    def _():
        pad = jnp.zeros((half - kn_ref.shape[0], DA_V), F32)
        zero_half = jnp.zeros((half, DA_V), F32)
        kn = wide(jnp.concatenate([kn_ref[...], pad, zero_half], axis=0))
        vn = wide(jnp.concatenate([vn_ref[...], pad, zero_half], axis=0))
        online(1, lambda u: kn, lambda u: vn, bias_new_ref[...])
        lam = _lam_value(lamv_ref, lam_init)
        l_all = l_sc[...]
        o = acc_sc[...] / to_rows(l_all + pltpu.roll(l_all, q_rows, axis=1))
        for h in range(DA_HEADS):
            base = 2 * t_new * h
            od = o[base:base + t_new] - lam * o[base + t_new:base + 2 * t_new]
            o_ref[:, h * DA_V:(h + 1) * DA_V] = (
                _rmsnorm(od, subln_ref[...], SUBLN_EPS) * (1.0 - lam_init))


def _attn_paged(page_table, lamv, subln, q, k_new, v_new, cache_k, cache_v, *,
                pages_per_step, pool_offset, lam_init):
    nb, n_pages = page_table.shape
    n, d_diff = q.shape
    t_new = n // nb
    page, heads = cache_k.shape[-3], cache_k.shape[-2]
    n_rows = page * heads
    q_rows = 2 * t_new * heads
    pp = pages_per_step
    ck = cache_k.reshape(-1, n_rows, DA_V)
    cv = cache_v.reshape(-1, n_rows, DA_V)
    kn = k_new.reshape(n * heads, DA_V)
    vn = v_new.reshape(n * heads, DA_V)
    assert 2 * q_rows == LANES and n_rows % (2 * SUBLANES) == 0 and t_new * heads <= n_rows // 2
    half = n_rows // 2
    r = jnp.arange(half)[:, None]
    c = jnp.arange(LANES)[None, :]
    same_head = (r % heads) == ((c % q_rows) // (2 * t_new))
    bias = jnp.where(same_head, 0.0, NEG).astype(F32)
    visible = same_head & (c < q_rows) & (r < t_new * heads) & (r // heads <= c % t_new)
    bias_new = jnp.where(visible, 0.0, NEG).astype(F32)
    seq_spec = pl.BlockSpec((t_new, d_diff), lambda b, g, pt: (b, 0))
    new_spec = pl.BlockSpec((t_new * heads, DA_V), lambda b, g, pt: (b, 0))

    def page_spec(u):
        return pl.BlockSpec((None, n_rows, DA_V),
                            lambda b, g, pt: (pool_offset + pt[b, g * pp + u], 0, 0))

    const = lambda shape: pl.BlockSpec(shape, lambda b, g, pt: (0,) * len(shape))
    grid_spec = pltpu.PrefetchScalarGridSpec(
        num_scalar_prefetch=1,
        grid=(nb, n_pages // pp),
        in_specs=[const(lamv.shape), const(subln.shape), const(bias.shape), const(bias_new.shape),
                  seq_spec, new_spec, new_spec]
        + [page_spec(u) for u in range(pp)] + [page_spec(u) for u in range(pp)],
        out_specs=seq_spec,
        scratch_shapes=[pltpu.VMEM((half, pp * LANES), F32),
                        pltpu.VMEM((1, LANES), F32),
                        pltpu.VMEM((1, LANES), F32),
                        pltpu.VMEM((q_rows, DA_V), F32)],
    )
    return pl.pallas_call(
        functools.partial(_attn_paged_kernel, pages_per_step=pp, t_new=t_new, lam_init=lam_init),
        grid_spec=grid_spec,
        out_shape=jax.ShapeDtypeStruct((n, d_diff), F32),
        compiler_params=_params("arbitrary", "arbitrary"),
        name="attn_paged",
    )(page_table, lamv, subln, bias, bias_new, q, kn, vn, *([ck] * pp), *([cv] * pp))


def _seg_sum(x, ones_pair):
    outs = []
    for p in range(x.shape[1] // RW_PAIR):
        hi, lo = _split2(x[:, p * RW_PAIR:(p + 1) * RW_PAIR])
        outs.append(_dot(hi, ones_pair) + _dot(lo, ones_pair))
    return jnp.concatenate(outs, axis=1)


def _round_robin(tasks):
    results = [None] * len(tasks)
    active = list(enumerate(tasks))
    while active:
        still = []
        for i, task in active:
            try:
                next(task)
                still.append((i, task))
            except StopIteration as stop:
                results[i] = stop.value
        active = still
    return results


def _tri_inverse(mats, c):
    row = lax.broadcasted_iota(jnp.int32, (c, c), 0)
    col = lax.broadcasted_iota(jnp.int32, (c, c), 1)
    eye = jnp.where(row == col, 1.0, 0.0).astype(F32)
    power = list(mats)
    factors = [[(eye + a).astype(BF16)] for a in mats]
    n = 1
    while 2 * n < c:
        yield
        power = [_dot(pb, pb) for pb in (p.astype(BF16) for p in power)]
        for fs, p in zip(factors, power):
            fs.append((eye + p).astype(BF16))
        n *= 2
    while len(factors[0]) > 1:
        yield
        nxt = []
        for fs in factors:
            prod = [_dot(fs[i], fs[i + 1]).astype(BF16) for i in range(0, len(fs) - 1, 2)]
            nxt.append(prod + ([fs[-1]] if len(fs) % 2 else []))
        factors = nxt
    return [fs[0] for fs in factors]


def _rwkv_kernel(zm_ref, zl_ref, pm_ref, pl_ref, s0_ref, mum_ref, mul_ref, w0_ref, a0_ref,
                 kk_ref, ka_ref, rk_ref, lng_ref, lnb_ref, ww_ref, wa_ref, wg_ref, ones_ref,
                 o_ref, sout_ref, cm_sc, cl_sc, h_sc, *, g_blk, t_blk, chunk, d_rw):
    c = pl.program_id(1)
    n_pairs = d_rw // RW_PAIR

    @pl.when(c == 0)
    def _():
        cm_sc[...] = pm_ref[...]
        cl_sc[...] = pl_ref[...]
        h_sc[...] = s0_ref[...]

    def shift_mix(z_ref, carry_ref, mu):
        out = []
        for g in range(g_blk):
            z = z_ref[g]
            rows = lax.broadcasted_iota(jnp.int32, z.shape, 0)
            zs = jnp.where(rows == 0, carry_ref[g], pltpu.roll(z, 1, axis=0))
            carry_ref[g] = z[t_blk - 1:t_blk, :]
            out.append(z + (zs - z) * mu)
        return jnp.concatenate(out, axis=0) if g_blk > 1 else out[0]

    zmix = shift_mix(zm_ref, cm_sc, mum_ref[...])
    zlm = shift_mix(zl_ref, cl_sc, mul_ref[...])
    r, k, v = zmix[:, :d_rw], zmix[:, d_rw:2 * d_rw], zmix[:, 2 * d_rw:]

    ones_bd = ones_ref[...]
    u = w0_ref[...] + _dot(jnp.tanh(zlm).astype(BF16), ww_ref[...])
    softplus = jnp.maximum(-u, 0.0) + jnp.log(1.0 + jnp.exp(-jnp.abs(u)))
    lw = -jnp.exp(-softplus - 0.5)
    a = _sigmoid(a0_ref[...] + _dot(zlm.astype(BF16), wa_ref[...]))
    gate = _dot(_sigmoid(zlm).astype(BF16), wg_ref[...])
    kk = k * kk_ref[...]
    kk = kk / jnp.maximum(jnp.sqrt(_seg_sum(kk * kk, ones_bd)), 1e-12)
    k2 = k * (1.0 + (a - 1.0) * ka_ref[...])
    av = -kk
    bv = kk * a
    bonus = _seg_sum(r * k2 * rk_ref[...], ones_bd) * v

    row = lax.broadcasted_iota(jnp.int32, (chunk, chunk), 0)
    col = lax.broadcasted_iota(jnp.int32, (chunk, chunk), 1)
    tri = jnp.where(row >= col, 1.0, 0.0).astype(BF16)
    zpad = jnp.zeros((chunk - t_blk, d_rw), F32) if chunk > t_blk else None

    def seq_rows(x, g):
        xg = x[g * t_blk:(g + 1) * t_blk]
        return xg if zpad is None else jnp.concatenate([xg, zpad], axis=0)

    lane = lax.broadcasted_iota(jnp.int32, (1, RW_PAIR), 1)
    first = lane < RW_HEAD
    prow = lax.broadcasted_iota(jnp.int32, (RW_PAIR, RW_PAIR), 0)
    pcol = lax.broadcasted_iota(jnp.int32, (RW_PAIR, RW_PAIR), 1)
    same_head = (prow < RW_HEAD) == (pcol < RW_HEAD)
    row2 = lax.broadcasted_iota(jnp.int32, (2 * chunk, chunk), 0)
    col2 = lax.broadcasted_iota(jnp.int32, (2 * chunk, chunk), 1)
    mask2 = jnp.where(row2 < chunk, row2, row2 - chunk + 1) > col2
    zero_b = jnp.zeros((), BF16)
    upd_rows = -(-2 * chunk // RW_PAIR) * RW_PAIR
    upd_pad = jnp.zeros((upd_rows - 2 * chunk, RW_PAIR), F32)

    def chunk_pair(g, p, at, rt, bt, kt, bh, kh, v_c, vb, gl):
        sl = slice(p * RW_PAIR, (p + 1) * RW_PAIR)
        lhs = jnp.concatenate([at[:, sl], rt[:, sl]], axis=0)
        h_old = h_sc[g, p]
        hh = _dot(lhs, h_old.astype(BF16))
        ab, ak = [], []
        for e in range(2):
            sel = first if e == 0 else jnp.logical_not(first)
            le = jnp.where(sel, lhs, zero_b)
            ab.append(jnp.where(mask2, _dot_nt(le, bt[:, sl]), 0.0))
            ak.append(jnp.where(mask2, _dot_nt(le, kt[:, sl]), 0.0))
        yield
        xk = [_dot(x.astype(BF16), vb[:, sl]) for x in ak]
        xb = [x[chunk:].astype(BF16) for x in ab]
        tinv = yield from _tri_inverse([x[:chunk] for x in ab], chunk)
        xkv = jnp.where(first, xk[0], xk[1])
        rhs = (hh[:chunk] + xkv[:chunk]).astype(BF16)
        yield
        uu = jnp.where(first, _dot(tinv[0], rhs), _dot(tinv[1], rhs))
        ub = uu.astype(BF16)
        yield
        y = hh[chunk:] + xkv[chunk:] + jnp.where(first, _dot(xb[0], ub), _dot(xb[1], ub))
        pieces_l = [bh[:, sl], kh[:, sl]]
        pieces_r = [uu, v_c[:, sl]]
        if upd_rows > 2 * chunk:
            pieces_l.append(upd_pad)
            pieces_r.append(upd_pad)
        lhs_t = jnp.concatenate(pieces_l, axis=0).T.astype(BF16)
        upd = _dot(lhs_t, jnp.concatenate(pieces_r, axis=0).astype(BF16))
        w_all = jnp.exp(jnp.broadcast_to(gl[:, sl], (RW_PAIR, RW_PAIR))).T
        h_sc[g, p] = h_old * w_all + jnp.where(same_head, upd, 0.0)
        return y[:t_blk]

    tasks = []
    for g in range(g_blk):
        lw_g = seq_rows(lw, g)
        gc = functools.reduce(lambda x, y: x + y, [_dot(tri, part) for part in _split3(lw_g)])
        gl = gc[chunk - 1:chunk, :]
        w_inv = jnp.exp(-gc)
        w_rem = jnp.exp(gl - gc)
        r_g, k_g, v_g, a_g, b_g = (seq_rows(x, g) for x in (r, k2, v, av, bv))
        at = (a_g * jnp.exp(gc - lw_g)).astype(BF16)
        rt = (r_g * jnp.exp(gc)).astype(BF16)
        bt = (b_g * w_inv).astype(BF16)
        kt = (k_g * w_inv).astype(BF16)
        bh = b_g * w_rem
        kh = k_g * w_rem
        vb = v_g.astype(BF16)
        tasks += [chunk_pair(g, p, at, rt, bt, kt, bh, kh, v_g, vb, gl) for p in range(n_pairs)]
    ys = _round_robin(tasks)
    y_rows = [jnp.concatenate(ys[g * n_pairs:(g + 1) * n_pairs], axis=1) for g in range(g_blk)]

    y = jnp.concatenate(y_rows, axis=0) if g_blk > 1 else y_rows[0]
    inv_n = 1.0 / RW_HEAD
    mean = _seg_sum(y, ones_bd) * inv_n
    yc = y - mean
    var = _seg_sum(yc * yc, ones_bd) * inv_n
    o = yc * lax.rsqrt(var + GN_EPS) * lng_ref[...] + lnb_ref[...]
    out = ((o + bonus) * gate).astype(o_ref.dtype)
    for g in range(g_blk):
        o_ref[g] = out[g * t_blk:(g + 1) * t_blk]

    @pl.when(c == pl.num_programs(1) - 1)
    def _():
        sout_ref[...] = h_sc[...]


def _rwkv(zm, zl, prev_m, prev_l, s0, prm, *, batch, seq, g_blk, t_blk, chunk):
    n, w_main = zm.shape
    d_rw = w_main // 3
    n_lora = zl.shape[1]
    n_pairs = d_rw // RW_PAIR
    row = lambda w: pl.BlockSpec((g_blk, t_blk, w), lambda b, c: (b, c, 0))
    per_b = lambda w: pl.BlockSpec((g_blk, 1, w), lambda b, c: (b, 0, 0))
    st_spec = pl.BlockSpec((g_blk, n_pairs, RW_PAIR, RW_PAIR), lambda b, c: (b, 0, 0, 0))
    vecs = [prm[name] for name in ("mu_m", "mu_l", "w0", "a0", "k_k", "k_a", "r_k", "ln_g", "ln_b")]
    mats = [prm["ww"], prm["wa"], prm["wg"], prm["ones_pair"]]
    o_b, s_new = pl.pallas_call(
        functools.partial(_rwkv_kernel, g_blk=g_blk, t_blk=t_blk, chunk=chunk, d_rw=d_rw),
        grid=(batch // g_blk, seq // t_blk),
        in_specs=[row(w_main), row(n_lora), per_b(w_main), per_b(n_lora), st_spec]
        + [_const_spec(x.shape) for x in vecs + mats],
        out_specs=[row(d_rw), st_spec],
        out_shape=[jax.ShapeDtypeStruct((batch, seq, d_rw), F32),
                   jax.ShapeDtypeStruct(s0.shape, F32)],
        scratch_shapes=[pltpu.VMEM((g_blk, 1, w_main), F32), pltpu.VMEM((g_blk, 1, n_lora), F32),
                        pltpu.VMEM((g_blk, n_pairs, RW_PAIR, RW_PAIR), F32)],
        compiler_params=_params("arbitrary", "arbitrary"),
        name="rwkv_mix",
    )(zm.reshape(batch, seq, w_main), zl.reshape(batch, seq, n_lora),
      prev_m.reshape(batch, 1, w_main), prev_l.reshape(batch, 1, n_lora), s0, *vecs, *mats)
    return o_b.reshape(n, d_rw), s_new


def _state_to_pairs(s):
    b, h, n, _ = s.shape
    st = jnp.swapaxes(s, -1, -2).reshape(b, h // 2, 2, n, n)
    eye = jnp.eye(2, dtype=s.dtype)
    bd = st[:, :, :, :, None, :] * eye[None, None, :, None, :, None]
    return bd.reshape(b, h // 2, 2 * n, 2 * n)


def _pairs_to_state(hp):
    b, p, n2, _ = hp.shape
    n = n2 // 2
    h6 = hp.reshape(b, p, 2, n, 2, n)
    diag = jnp.stack([h6[:, :, 0, :, 0, :], h6[:, :, 1, :, 1, :]], axis=2)
    return jnp.swapaxes(diag, -1, -2).reshape(b, 2 * p, n, n)


def _ffn_kernel(x_ref, oa_ref, ob_ref, cp_ref, woa_ref, wob_ref, nf_ref, wup_ref, cw_ref,
                cb_ref, wdn_ref, nfin_ref, y_ref, cs_ref, carry_sc, *, d_ff, cw, carried,
                apply_final):
    t = pl.program_id(1)
    tm = x_ref.shape[0]
    groups = tm // SUBLANES

    if carried:
        @pl.when(t == 0)
        def _():
            carry_sc[...] = jnp.zeros(carry_sc.shape, F32)
            carry_sc[:, SUBLANES - (CONV_W - 1):, :] = cp_ref[...]

    x1 = (x_ref[...] + _dot(oa_ref[...].astype(BF16), woa_ref[...])
          + _dot(ob_ref[...].astype(BF16), wob_ref[...]))
    h = _rmsnorm(x1, nf_ref[...], RMS_EPS).astype(BF16)
    t8 = lax.broadcasted_iota(jnp.int32, (groups, SUBLANES, cw), 1)

    def up_proj(ci):
        return [_dot(h, wup_ref[:, col0:col0 + cw]) for col0 in (ci * cw, d_ff + ci * cw)]

    def conv(up, col0):
        cols = slice(col0, col0 + cw)
        up = up.reshape(groups, SUBLANES, cw)
        if carried:
            prev = carry_sc[:, :, cols]
            if groups > 1:
                prev = jnp.concatenate([prev, up[:-1]], axis=0)
            p6, p7 = prev[:, 6:7, :], prev[:, 7:8, :]
            carry_sc[:, :, cols] = up[groups - 1:]
            cs_ref[:, :, cols] = up[groups - 1:, SUBLANES - (CONV_W - 1):, :]
        else:
            p6, p7 = cp_ref[:, 0:1, cols], cp_ref[:, 1:2, cols]
            cs_ref[:, :, cols] = up[:, SUBLANES - (CONV_W - 1):, :]
        m1 = jnp.where(t8 == 0, p7, pltpu.roll(up, 1, axis=1))
        m2 = jnp.where(t8 == 0, p6, jnp.where(t8 == 1, p7, pltpu.roll(up, 2, axis=1)))
        w = cw_ref[:, cols]
        out = cb_ref[:, cols] + m2 * w[0:1] + m1 * w[1:2] + up * w[2:3]
        return out.reshape(tm, cw)

    n_chunks = d_ff // cw
    acc = jnp.zeros((tm, x_ref.shape[1]), F32)
    ups = up_proj(0)
    for ci in range(n_chunks):
        nxt = up_proj(ci + 1) if ci + 1 < n_chunks else None
        gate = conv(ups[0], ci * cw)
        val = conv(ups[1], d_ff + ci * cw)
        act = (gate * _sigmoid(gate) * val).astype(BF16)
        acc = acc + _dot(act, wdn_ref[ci * cw:(ci + 1) * cw, :])
        ups = nxt
    x2 = x1 + acc
    y_ref[...] = _rmsnorm(x2, nfin_ref[...], RMS_EPS) if apply_final else x2


def _ffn(x, oa, ob, conv_prev, woa, wob, nf, wup, cwt, cb, wdn, nfin, *, batch, seq, tm, cw,
         apply_final):
    n, d = x.shape
    d_mix = oa.shape[1]
    d_ff2 = wup.shape[1]
    d_ff = d_ff2 // 2
    carried = seq >= tm
    if carried:
        nt = seq // tm
        grid = (batch, nt)
        row = lambda w: pl.BlockSpec((tm, w), lambda b, t: (b * nt + t, 0))
        cs_spec = pl.BlockSpec((1, CONV_W - 1, d_ff2), lambda b, t: (b, 0, 0))
        cp_spec = cs_spec
    else:
        seqs = tm // seq
        grid = (batch // seqs, 1)
        row = lambda w: pl.BlockSpec((tm, w), lambda b, t: (b, 0))
        cs_spec = pl.BlockSpec((seqs, CONV_W - 1, d_ff2), lambda b, t: (b, 0, 0))
        cp_spec = cs_spec
    return pl.pallas_call(
        functools.partial(_ffn_kernel, d_ff=d_ff, cw=cw, carried=carried,
                          apply_final=apply_final),
        grid=grid,
        in_specs=[row(d), row(d_mix), row(d_mix), cp_spec, _const_spec(woa.shape),
                  _const_spec(wob.shape), _const_spec(nf.shape), _const_spec(wup.shape),
                  _const_spec(cwt.shape), _const_spec(cb.shape), _const_spec(wdn.shape),
                  _const_spec(nfin.shape)],
        out_specs=[row(d), cs_spec],
        out_shape=[jax.ShapeDtypeStruct((n, d), F32),
                   jax.ShapeDtypeStruct((batch, CONV_W - 1, d_ff2), F32)],
        scratch_shapes=[pltpu.VMEM((1, SUBLANES, d_ff2), F32)],
        compiler_params=_params("arbitrary", "arbitrary"),
        name="out_proj_ffn",
    )(x, oa, ob, conv_prev, woa, wob, nf, wup, cwt, cb, wdn, nfin)


def _pick_tile(n, target):
    t = min(n, target)
    while n % t:
        t //= 2
    return t


def kernel(x_prompt, x_sample, cache_k, cache_v, state_wkv, state_shift, state_conv, page_table, norm_mix, w_in, lam_q1, lam_k1, lam_q2, lam_k2, subln, rw_mu, rw_w0, rw_w_up, rw_a0, rw_a_up, rw_g_up, rw_k_k, rw_k_a, rw_r_k, rw_ln_g, rw_ln_b, w_out, norm_ffn, w_up, conv_w, conv_b, w_down, norm_final):
    depth = w_in.shape[0]
    bp, tp, d = x_prompt.shape
    bs, ts, _ = x_sample.shape
    d_rw = rw_w0.shape[1]
    d_diff = DA_HEADS * DA_V
    n_wl, n_al, n_gl = rw_w_up.shape[1], rw_a_up.shape[1], rw_g_up.shape[1]
    n_lora = n_wl + n_al + n_gl
    n_rw_cols = 3 * d_rw + n_lora

    xp = x_prompt.reshape(bp * tp, d)
    xs = x_sample.reshape(bs * ts, d)
    head_of = jnp.arange(RW_PAIR) // RW_HEAD
    ones_pair = (head_of[:, None] == head_of[None, :]).astype(BF16)
    row2 = lambda vec: vec.reshape(1, -1)

    outs_p = [[] for _ in range(5)]
    outs_s = [[] for _ in range(5)]
    for layer in range(depth):
        lam_init = 0.8 - 0.6 * math.exp(-0.3 * layer)
        last = layer == depth - 1
        wl = w_in[layer]
        wa = wl[:, :3 * d_diff].astype(BF16)
        wr = wl[:, 3 * d_diff:3 * d_diff + 3 * d_rw].astype(BF16)
        wlo = wl[:, 3 * d_diff + 3 * d_rw:].astype(BF16)
        lamv = jnp.stack([lam_q1[layer], lam_k1[layer], lam_q2[layer], lam_k2[layer]])
        sub = row2(subln[layer])
        mu = rw_mu[layer]
        zpad = lambda rows: jnp.zeros((rows, d_rw), BF16)
        prm = {
            "mu_m": row2(mu[:3 * d_rw]), "mu_l": row2(mu[3 * d_rw:]),
            "w0": row2(rw_w0[layer]), "a0": row2(rw_a0[layer]),
            "k_k": row2(rw_k_k[layer]), "k_a": row2(rw_k_a[layer]),
            "r_k": row2(rw_r_k[layer]), "ln_g": row2(rw_ln_g[layer]), "ln_b": row2(rw_ln_b[layer]),
            "ww": jnp.concatenate([rw_w_up[layer].astype(BF16), zpad(n_al + n_gl)], axis=0),
            "wa": jnp.concatenate([zpad(n_wl), rw_a_up[layer].astype(BF16), zpad(n_gl)], axis=0),
            "wg": jnp.concatenate([zpad(n_wl + n_al), rw_g_up[layer].astype(BF16)], axis=0),
            "ones_pair": ones_pair,
        }
        wo = w_out[layer].astype(BF16)
        ffn_w = (wo[:d_diff], wo[d_diff:], row2(norm_ffn[layer]), w_up[layer].astype(BF16),
                 conv_w[layer], row2(conv_b[layer]), w_down[layer].astype(BF16), row2(norm_final))

        def run(x, batch, seq, prev_shift, s0, conv_prev, attn_fn):
            n = batch * seq
            q, k, v, kb, vb, zm, zl = _norm_proj(x, row2(norm_mix[layer]), wa, wr, wlo,
                                                 tm=_pick_tile(n, 256))
            o_a = attn_fn(q, k, v, kb, vb)
            t_blk = min(seq, 64)
            o_b, s_new = _rwkv(zm, zl, prev_shift[:, :3 * d_rw], prev_shift[:, 3 * d_rw:],
                               s0, prm, batch=batch, seq=seq,
                               g_blk=_pick_tile(batch, 4 if t_blk == 64 else 8),
                               t_blk=t_blk, chunk=max(t_blk, 16))
            tm = _pick_tile(n, 512 if seq >= 512 else 256)
            y, conv_new = _ffn(x, o_a, o_b, conv_prev, *ffn_w, batch=batch, seq=seq, tm=tm,
                               cw=256, apply_final=last)
            shift_new = jnp.concatenate([zm.reshape(batch, seq, -1)[:, -1],
                                         zl.reshape(batch, seq, -1)[:, -1]], axis=-1)
            return (y, k.reshape(batch, seq, DA_HEADS, DA_V), v.reshape(batch, seq, DA_HEADS, DA_V),
                    _pairs_to_state(s_new), shift_new, conv_new)

        prompt_attn = lambda q, k, v, kb, vb: _attn_prompt(
            lamv, sub, q, kb, vb, batch=bp, seq=tp, tq=_pick_tile(tp, 256), lam_init=lam_init)
        xp, kp, vp, wp, sp, cp = run(
            xp, bp, tp, jnp.zeros((bp, n_rw_cols), F32),
            jnp.zeros((bp, d_rw // RW_PAIR, RW_PAIR, RW_PAIR), F32),
            jnp.zeros((bp, CONV_W - 1, w_up.shape[2]), F32), prompt_attn)
        sample_attn = lambda q, k, v, kb, vb: _attn_paged(
            page_table, lamv, sub, q, k, v, cache_k, cache_v,
            pages_per_step=_pick_tile(page_table.shape[1], 16),
            pool_offset=layer * cache_k.shape[1], lam_init=lam_init)
        xs, ks, vs, ws, ss, cs = run(
            xs, bs, ts, state_shift[layer], _state_to_pairs(state_wkv[layer]),
            state_conv[layer], sample_attn)
        for acc, val in zip(outs_p, (kp, vp, wp, sp, cp)):
            acc.append(val)
        for acc, val in zip(outs_s, (ks, vs, ws, ss, cs)):
            acc.append(val)

    y_prompt = xp.reshape(bp, tp, d)
    y_sample = xs.reshape(bs, ts, d)
    return (y_prompt, y_sample, *[jnp.stack(o) for o in outs_p], *[jnp.stack(o) for o in outs_s])
```

```python
import functools
import math

import jax
import jax.numpy as jnp
from jax import lax
from jax.experimental import pallas as pl
from jax.experimental.pallas import tpu as pltpu

F32 = jnp.float32
BF16 = jnp.bfloat16

LANES = 128
SUBLANES = 8
VMEM_LIMIT_BYTES = 56 * 1024 * 1024

RMS_EPS = 1e-6
SUBLN_EPS = 1e-5
GN_EPS = 64e-5
NEG = -1e30

DA_HEADS = 4
DA_V = 128
DA_QK = 64
RW_HEAD = 64
RW_PAIR = 2 * RW_HEAD
CONV_W = 3


def _params(*sem):
    return pltpu.CompilerParams(dimension_semantics=sem, vmem_limit_bytes=VMEM_LIMIT_BYTES)


def _const_spec(shape):
    zeros = (0,) * len(shape)
    return pl.BlockSpec(shape, lambda *_: zeros, pipeline_mode=pl.Buffered(1))


def _dot(a, b):
    return jnp.dot(a, b, preferred_element_type=F32)


def _dot_nt(a, b):
    return lax.dot_general(a, b, (((1,), (1,)), ((), ())), preferred_element_type=F32)


def _split2(x):
    hi = x.astype(BF16)
    lo = (x - hi.astype(F32)).astype(BF16)
    return hi, lo


def _split3(x):
    hi = x.astype(BF16)
    r1 = x - hi.astype(F32)
    mid = r1.astype(BF16)
    lo = (r1 - mid.astype(F32)).astype(BF16)
    return hi, mid, lo


def _rmsnorm(x, g, eps):
    return x * lax.rsqrt(jnp.mean(x * x, axis=-1, keepdims=True) + eps) * g


def _sigmoid(x):
    return 1.0 / (1.0 + jnp.exp(-x))


def _norm_proj_kernel(x_ref, g_ref, wa_ref, wr_ref, wl_ref,
                      q_ref, k_ref, v_ref, kb_ref, vb_ref, zm_ref, zl_ref, *, d_diff):
    h = _rmsnorm(x_ref[...], g_ref[...], RMS_EPS).astype(BF16)
    pa = _dot(h, wa_ref[...])
    q_ref[...] = pa[:, :d_diff] * (DA_QK ** -0.5)
    k = pa[:, d_diff:2 * d_diff]
    v = pa[:, 2 * d_diff:]
    for head in range(d_diff // DA_V):
        k_ref[:, head, :] = k[:, head * DA_V:(head + 1) * DA_V]
        v_ref[:, head, :] = v[:, head * DA_V:(head + 1) * DA_V]
    kb_ref[...] = k.astype(BF16)
    vb_ref[...] = v.astype(BF16)
    zm_ref[...] = _dot(h, wr_ref[...])
    zl_ref[...] = _dot(h, wl_ref[...])


def _norm_proj(x, g, wa, wr, wl, *, tm):
    n, d = x.shape
    d_diff = wa.shape[1] // 3
    n_rkv, n_lora = wr.shape[1], wl.shape[1]
    heads = d_diff // DA_V
    row = lambda w: pl.BlockSpec((tm, w), lambda i: (i, 0))
    by_head = pl.BlockSpec((tm, heads, DA_V), lambda i: (i, 0, 0))
    return pl.pallas_call(
        functools.partial(_norm_proj_kernel, d_diff=d_diff),
        grid=(n // tm,),
        in_specs=[row(d), _const_spec((1, d)), _const_spec(wa.shape), _const_spec(wr.shape),
                  _const_spec(wl.shape)],
        out_specs=[row(d_diff), by_head, by_head, row(d_diff), row(d_diff),
                   row(n_rkv), row(n_lora)],
        out_shape=[jax.ShapeDtypeStruct((n, d_diff), F32),
                   jax.ShapeDtypeStruct((n, heads, DA_V), F32),
                   jax.ShapeDtypeStruct((n, heads, DA_V), F32),
                   jax.ShapeDtypeStruct((n, d_diff), BF16),
                   jax.ShapeDtypeStruct((n, d_diff), BF16),
                   jax.ShapeDtypeStruct((n, n_rkv), F32),
                   jax.ShapeDtypeStruct((n, n_lora), F32)],
        compiler_params=_params("arbitrary"),
        name="norm_proj",
    )(x, g, wa, wr, wl)


def _lam_value(lamv_ref, lam_init):
    lv = lamv_ref[...]
    s1 = jnp.sum(lv[0:1] * lv[1:2], axis=-1, keepdims=True)
    s2 = jnp.sum(lv[2:3] * lv[3:4], axis=-1, keepdims=True)
    return jnp.exp(s1) - jnp.exp(s2) + lam_init


def _stack_maps(q):
    lane = lax.broadcasted_iota(jnp.int32, q.shape, 1)
    zero = jnp.zeros_like(q)
    return jnp.concatenate([jnp.where(lane < DA_QK, q, zero),
                            jnp.where(lane >= DA_QK, q, zero)], axis=0)


def _attn_prompt_kernel(lamv_ref, subln_ref, bias_ref, q_ref, k_ref, v_ref, o_ref,
                        m_sc, l_sc, acc_sc, *, tq, heads, lam_init):
    i = pl.program_id(1)
    q = q_ref[...]
    lanes = [slice(h * DA_V, (h + 1) * DA_V) for h in range(heads)]
    qs = [_stack_maps(q[:, sl]).astype(BF16) for sl in lanes]
    m_sc[...] = jnp.full(m_sc.shape, NEG, F32)
    l_sc[...] = jnp.zeros(l_sc.shape, F32)
    acc_sc[...] = jnp.zeros(acc_sc.shape, F32)

    def update(start, width, bias):
        rows = pl.ds(start, width)
        scores = [_dot_nt(qs[h], k_ref[rows, lanes[h]]) for h in range(heads)]
        for h in range(heads):
            s = scores[h] if bias is None else scores[h] + bias
            blocks = [s[:, c:c + LANES] for c in range(0, width, LANES)]
            m = m_sc[h]
            row_max = jnp.max(functools.reduce(jnp.maximum, blocks), axis=-1, keepdims=True)
            m_new = jnp.maximum(m, jnp.broadcast_to(row_max, m.shape))
            corr = jnp.exp(m - m_new)
            ps = [jnp.exp(blk - m_new) for blk in blocks]
            m_sc[h] = m_new
            l_sc[h] = l_sc[h] * corr + functools.reduce(lambda a, b: a + b, ps)
            p = jnp.concatenate([x.astype(BF16) for x in ps], axis=1)
            acc_sc[h] = acc_sc[h] * corr + _dot(p, v_ref[rows, lanes[h]])

    @pl.loop(0, i // 2)
    def _(j):
        update(pl.multiple_of(j * 2 * tq, 2 * tq), 2 * tq, None)

    @pl.when(i % 2 == 1)
    def _():
        update(pl.multiple_of((i - 1) * tq, tq), tq, None)

    update(pl.multiple_of(i * tq, tq), tq, bias_ref[...])
    lam = _lam_value(lamv_ref, lam_init)
    for h in range(heads):
        o = acc_sc[h] / jnp.sum(l_sc[h], axis=-1, keepdims=True)
        od = o[:tq] - lam * o[tq:]
        o_ref[:, lanes[h]] = (_rmsnorm(od, subln_ref[...], SUBLN_EPS)
                              * (1.0 - lam_init)).astype(o_ref.dtype)


def _attn_prompt(lamv, subln, q, kb, vb, *, batch, seq, tq, lam_init):
    n, d_diff = q.shape
    heads = d_diff // DA_V
    nq = seq // tq
    kb3 = kb.reshape(batch, seq, d_diff)
    vb3 = vb.reshape(batch, seq, d_diff)
    r = jnp.arange(2 * tq)[:, None] % tq
    causal_bias = jnp.where(jnp.arange(tq)[None, :] <= r, 0.0, NEG).astype(F32)
    qspec = pl.BlockSpec((tq, d_diff), lambda b, i: (b * nq + i, 0))
    kvspec = pl.BlockSpec((None, seq, d_diff), lambda b, i: (b, 0, 0))
    return pl.pallas_call(
        functools.partial(_attn_prompt_kernel, tq=tq, heads=heads, lam_init=lam_init),
        grid=(batch, nq),
        in_specs=[_const_spec(lamv.shape), _const_spec(subln.shape),
                  _const_spec(causal_bias.shape), qspec, kvspec, kvspec],
        out_specs=qspec,
        out_shape=jax.ShapeDtypeStruct((n, d_diff), BF16),
        scratch_shapes=[pltpu.VMEM((heads, 2 * tq, LANES), F32),
                        pltpu.VMEM((heads, 2 * tq, LANES), F32),
                        pltpu.VMEM((heads, 2 * tq, DA_V), F32)],
        compiler_params=_params("arbitrary", "arbitrary"),
        name="attn_prompt",
    )(lamv, subln, causal_bias, q, kb3, vb3)


def _attn_paged_kernel(pt_ref, lamv_ref, subln_ref, bias_ref, bias_new_ref, q_ref, kn_ref, vn_ref,
                       *rest, pages_per_step, t_new, lam_init):
    del pt_ref
    pp = pages_per_step
    k_refs, v_refs = rest[:pp], rest[pp:2 * pp]
    o_ref, s_sc, m_sc, l_sc, acc_sc = rest[2 * pp:]
    g = pl.program_id(1)
    half = k_refs[0].shape[0] // 2
    q_rows = 2 * t_new * DA_HEADS
    group = 4

    @pl.when(g == 0)
    def _():
        m_sc[...] = jnp.full(m_sc.shape, NEG, F32)
        l_sc[...] = jnp.zeros(l_sc.shape, F32)
        acc_sc[...] = jnp.zeros(acc_sc.shape, F32)

    q = q_ref[...]
    q_all = jnp.concatenate(
        [_stack_maps(q[:, h * DA_V:(h + 1) * DA_V]) for h in range(DA_HEADS)], axis=0)
    zq = jnp.zeros_like(q_all)
    wq = jnp.concatenate([jnp.concatenate([q_all, zq], axis=1),
                          jnp.concatenate([zq, q_all], axis=1)], axis=0).astype(BF16)

    def wide(x):
        return jnp.concatenate([x[:half], x[half:]], axis=1).astype(BF16)

    def to_rows(vec):
        return jnp.broadcast_to(vec, (LANES, LANES)).T[:q_rows]

    def online(n_pg, load_k, load_v, bias):
        for u in range(n_pg):
            s_sc[:, u * LANES:(u + 1) * LANES] = _dot_nt(load_k(u), wq) + bias
        for u0 in range(0, n_pg, group):
            pages = range(u0, min(u0 + group, n_pg))
            m_old = m_sc[...]
            mx = functools.reduce(jnp.maximum, [s_sc[:, u * LANES:(u + 1) * LANES] for u in pages])
            cm = jnp.max(mx, axis=0, keepdims=True)
            m_new = jnp.maximum(m_old, jnp.maximum(cm, pltpu.roll(cm, q_rows, axis=1)))
            corr = jnp.exp(m_old - m_new)
            psum = None
            pv = None
            for u in pages:
                p = jnp.exp(s_sc[:, u * LANES:(u + 1) * LANES] - m_new)
                psum = p if psum is None else psum + p
                d = _dot(p.T.astype(BF16), load_v(u))
                pv = d if pv is None else pv + d
            m_sc[...] = m_new
            l_sc[...] = l_sc[...] * corr + jnp.sum(psum, axis=0, keepdims=True)
            acc_sc[...] = acc_sc[...] * to_rows(corr) + pv[:q_rows, :DA_V] + pv[q_rows:, DA_V:]

    online(pp, lambda u: wide(k_refs[u][...]), lambda u: wide(v_refs[u][...]), bias_ref[...])

    @pl.when(g == pl.num_programs(1) - 1)
    def _():
        pad = jnp.zeros((half - kn_ref.shape[0], DA_V), F32)
        zero_half = jnp.zeros((half, DA_V), F32)
        kn = wide(jnp.concatenate([kn_ref[...], pad, zero_half], axis=0))
        vn = wide(jnp.concatenate([vn_ref[...], pad, zero_half], axis=0))
        online(1, lambda u: kn, lambda u: vn, bias_new_ref[...])
        lam = _lam_value(lamv_ref, lam_init)
        l_all = l_sc[...]
        o = acc_sc[...] / to_rows(l_all + pltpu.roll(l_all, q_rows, axis=1))
        for h in range(DA_HEADS):
            base = 2 * t_new * h
            od = o[base:base + t_new] - lam * o[base + t_new:base + 2 * t_new]
            o_ref[:, h * DA_V:(h + 1) * DA_V] = (
                _rmsnorm(od, subln_ref[...], SUBLN_EPS) * (1.0 - lam_init))


def _attn_paged(page_table, lamv, subln, q, k_new, v_new, cache_k, cache_v, *,
                pages_per_step, pool_offset, lam_init):
    nb, n_pages = page_table.shape
    n, d_diff = q.shape
    t_new = n // nb
    page, heads = cache_k.shape[-3], cache_k.shape[-2]
    n_rows = page * heads
    q_rows = 2 * t_new * heads
    pp = pages_per_step
    ck = cache_k.reshape(-1, n_rows, DA_V)
    cv = cache_v.reshape(-1, n_rows, DA_V)
    kn = k_new.reshape(n * heads, DA_V)
    vn = v_new.reshape(n * heads, DA_V)
    assert 2 * q_rows == LANES and n_rows % (2 * SUBLANES) == 0 and t_new * heads <= n_rows // 2
    half = n_rows // 2
    r = jnp.arange(half)[:, None]
    c = jnp.arange(LANES)[None, :]
    same_head = (r % heads) == ((c % q_rows) // (2 * t_new))
    bias = jnp.where(same_head, 0.0, NEG).astype(F32)
    visible = same_head & (c < q_rows) & (r < t_new * heads) & (r // heads <= c % t_new)
    bias_new = jnp.where(visible, 0.0, NEG).astype(F32)
    seq_spec = pl.BlockSpec((t_new, d_diff), lambda b, g, pt: (b, 0))
    new_spec = pl.BlockSpec((t_new * heads, DA_V), lambda b, g, pt: (b, 0))

    def page_spec(u):
        return pl.BlockSpec((None, n_rows, DA_V),
                            lambda b, g, pt: (pool_offset + pt[b, g * pp + u], 0, 0))

    const = lambda shape: pl.BlockSpec(shape, lambda b, g, pt: (0,) * len(shape))
    grid_spec = pltpu.PrefetchScalarGridSpec(
        num_scalar_prefetch=1,
        grid=(nb, n_pages // pp),
        in_specs=[const(lamv.shape), const(subln.shape), const(bias.shape), const(bias_new.shape),
                  seq_spec, new_spec, new_spec]
        + [page_spec(u) for u in range(pp)] + [page_spec(u) for u in range(pp)],
        out_specs=seq_spec,
        scratch_shapes=[pltpu.VMEM((half, pp * LANES), F32),
                        pltpu.VMEM((1, LANES), F32),
                        pltpu.VMEM((1, LANES), F32),
                        pltpu.VMEM((q_rows, DA_V), F32)],
    )
    return pl.pallas_call(
        functools.partial(_attn_paged_kernel, pages_per_step=pp, t_new=t_new, lam_init=lam_init),
        grid_spec=grid_spec,
        out_shape=jax.ShapeDtypeStruct((n, d_diff), F32),
        compiler_params=_params("arbitrary", "arbitrary"),
        name="attn_paged",
    )(page_table, lamv, subln, bias, bias_new, q, kn, vn, *([ck] * pp), *([cv] * pp))


def _seg_sum(x, ones_seg):
    width = ones_seg.shape[0]
    outs = []
    for c0 in range(0, x.shape[1], width):
        hi, lo = _split2(x[:, c0:c0 + width])
        outs.append(_dot(hi, ones_seg) + _dot(lo, ones_seg))
    return jnp.concatenate(outs, axis=1)


def _round_robin(tasks):
    results = [None] * len(tasks)
    active = list(enumerate(tasks))
    while active:
        still = []
        for i, task in active:
            try:
                next(task)
                still.append((i, task))
            except StopIteration as stop:
                results[i] = stop.value
        active = still
    return results


def _tri_inverse(mats, c):
    row = lax.broadcasted_iota(jnp.int32, (c, c), 0)
    col = lax.broadcasted_iota(jnp.int32, (c, c), 1)
    eye = jnp.where(row == col, 1.0, 0.0).astype(F32)
    power = list(mats)
    factors = [[(eye + a).astype(BF16)] for a in mats]
    n = 1
    while 2 * n < c:
        yield
        power = [_dot(pb, pb) for pb in (p.astype(BF16) for p in power)]
        for fs, p in zip(factors, power):
            fs.append((eye + p).astype(BF16))
        n *= 2
    while len(factors[0]) > 1:
        yield
        nxt = []
        for fs in factors:
            prod = [_dot(fs[i], fs[i + 1]).astype(BF16) for i in range(0, len(fs) - 1, 2)]
            nxt.append(prod + ([fs[-1]] if len(fs) % 2 else []))
        factors = nxt
    return [fs[0] for fs in factors]


def _rwkv_kernel(zm_ref, zl_ref, pm_ref, pl_ref, s0_ref, mum_ref, mul_ref, w0_ref, a0_ref,
                 kk_ref, ka_ref, rk_ref, lng_ref, lnb_ref, ww_ref, wa_ref, wg_ref, ones_ref,
                 o_ref, sout_ref, cm_sc, cl_sc, h_sc, *, g_blk, t_blk, chunk, d_rw):
    c = pl.program_id(1)
    n_pairs = d_rw // RW_PAIR
    prow = lax.broadcasted_iota(jnp.int32, (RW_PAIR, RW_PAIR), 0)
    pcol = lax.broadcasted_iota(jnp.int32, (RW_PAIR, RW_PAIR), 1)
    same_head = (prow < RW_HEAD) == (pcol < RW_HEAD)
    eye = jnp.where(prow == pcol, 1.0, 0.0).astype(BF16)
    hrow = lax.broadcasted_iota(jnp.int32, (RW_HEAD, RW_HEAD), 0)
    hcol = lax.broadcasted_iota(jnp.int32, (RW_HEAD, RW_HEAD), 1)
    eye_head = jnp.where(hrow == hcol, 1.0, 0.0).astype(BF16)

    def transpose_exact(x, ident):
        return functools.reduce(lambda a, b: a + b, [_dot_nt(ident, part) for part in _split3(x)])

    @pl.when(c == 0)
    def _():
        cm_sc[...] = pm_ref[...]
        cl_sc[...] = pl_ref[...]
        for g in range(g_blk):
            for p in range(n_pairs):
                pair = s0_ref[g, 2 * p:2 * p + 2].reshape(RW_PAIR, RW_HEAD)
                kv = transpose_exact(pair, eye_head)
                h_sc[g, p] = jnp.where(same_head, jnp.concatenate([kv, kv], axis=0), 0.0)

    def shift_mix(z_ref, carry_ref, mu):
        out = []
        for g in range(g_blk):
            z = z_ref[g]
            rows = lax.broadcasted_iota(jnp.int32, z.shape, 0)
            zs = jnp.where(rows == 0, carry_ref[g], pltpu.roll(z, 1, axis=0))
            carry_ref[g] = z[t_blk - 1:t_blk, :]
            out.append(z + (zs - z) * mu)
        return jnp.concatenate(out, axis=0) if g_blk > 1 else out[0]

    zmix = shift_mix(zm_ref, cm_sc, mum_ref[...])
    zlm = shift_mix(zl_ref, cl_sc, mul_ref[...])
    r, k, v = zmix[:, :d_rw], zmix[:, d_rw:2 * d_rw], zmix[:, 2 * d_rw:]

    ones_bd = ones_ref[...]
    n_wa = ww_ref.shape[0]
    z_wa, z_g = zlm[:, :n_wa], zlm[:, n_wa:]
    u = w0_ref[...] + _dot(jnp.tanh(z_wa).astype(BF16), ww_ref[...])
    softplus = jnp.maximum(-u, 0.0) + jnp.log(1.0 + jnp.exp(-jnp.abs(u)))
    lw = -jnp.exp(-softplus - 0.5)
    a = _sigmoid(a0_ref[...] + _dot(z_wa.astype(BF16), wa_ref[...]))
    gate = _dot(_sigmoid(z_g).astype(BF16), wg_ref[...])
    kk = k * kk_ref[...]
    kk = kk / jnp.maximum(jnp.sqrt(_seg_sum(kk * kk, ones_bd)), 1e-12)
    k2 = k * (1.0 + (a - 1.0) * ka_ref[...])
    av = -kk
    bv = kk * a
    bonus = _seg_sum(r * k2 * rk_ref[...], ones_bd) * v

    row = lax.broadcasted_iota(jnp.int32, (chunk, chunk), 0)
    col = lax.broadcasted_iota(jnp.int32, (chunk, chunk), 1)
    tri = jnp.where(row >= col, 1.0, 0.0).astype(BF16)
    zpad = jnp.zeros((chunk - t_blk, d_rw), F32) if chunk > t_blk else None

    def seq_rows(x, g):
        xg = x[g * t_blk:(g + 1) * t_blk]
        return xg if zpad is None else jnp.concatenate([xg, zpad], axis=0)

    lane = lax.broadcasted_iota(jnp.int32, (1, RW_PAIR), 1)
    first = lane < RW_HEAD
    row2 = lax.broadcasted_iota(jnp.int32, (2 * chunk, chunk), 0)
    col2 = lax.broadcasted_iota(jnp.int32, (2 * chunk, chunk), 1)
    mask2 = jnp.where(row2 < chunk, row2, row2 - chunk + 1) > col2
    zero_b = jnp.zeros((), BF16)
    upd_rows = -(-2 * chunk // RW_PAIR) * RW_PAIR
    upd_pad = jnp.zeros((upd_rows - 2 * chunk, RW_PAIR), F32)

    def chunk_pair(g, p, at, rt, bt, kt, bh, kh, v_c, vb, gl):
        sl = slice(p * RW_PAIR, (p + 1) * RW_PAIR)
        lhs = jnp.concatenate([at[:, sl], rt[:, sl]], axis=0)
        h_old = h_sc[g, p]
        hh = _dot(lhs, h_old.astype(BF16))
        ab, ak = [], []
        for e in range(2):
            sel = first if e == 0 else jnp.logical_not(first)
            le = jnp.where(sel, lhs, zero_b)
            ab.append(jnp.where(mask2, _dot_nt(le, bt[:, sl]), 0.0))
            ak.append(jnp.where(mask2, _dot_nt(le, kt[:, sl]), 0.0))
        yield
        xk = [_dot(x.astype(BF16), vb[:, sl]) for x in ak]
        xb = [x[chunk:].astype(BF16) for x in ab]
        tinv = yield from _tri_inverse([x[:chunk] for x in ab], chunk)
        xkv = jnp.where(first, xk[0], xk[1])
        rhs = (hh[:chunk] + xkv[:chunk]).astype(BF16)
        yield
        uu = jnp.where(first, _dot(tinv[0], rhs), _dot(tinv[1], rhs))
        ub = uu.astype(BF16)
        yield
        y = hh[chunk:] + xkv[chunk:] + jnp.where(first, _dot(xb[0], ub), _dot(xb[1], ub))
        pieces_l = [bh[:, sl], kh[:, sl]]
        pieces_r = [uu, v_c[:, sl]]
        if upd_rows > 2 * chunk:
            pieces_l.append(upd_pad)
            pieces_r.append(upd_pad)
        lhs_t = jnp.concatenate(pieces_l, axis=0).T.astype(BF16)
        upd = _dot(lhs_t, jnp.concatenate(pieces_r, axis=0).astype(BF16))
        w_all = jnp.exp(jnp.broadcast_to(gl[:, sl], (RW_PAIR, RW_PAIR))).T
        h_sc[g, p] = h_old * w_all + jnp.where(same_head, upd, 0.0)
        return y[:t_blk]

    tasks = []
    for g in range(g_blk):
        lw_g = seq_rows(lw, g)
        gc = functools.reduce(lambda x, y: x + y, [_dot(tri, part) for part in _split3(lw_g)])
        gl = gc[chunk - 1:chunk, :]
        w_inv = jnp.exp(-gc)
        w_rem = jnp.exp(gl - gc)
        r_g, k_g, v_g, a_g, b_g = (seq_rows(x, g) for x in (r, k2, v, av, bv))
        at = (a_g * jnp.exp(gc - lw_g)).astype(BF16)
        rt = (r_g * jnp.exp(gc)).astype(BF16)
        bt = (b_g * w_inv).astype(BF16)
        kt = (k_g * w_inv).astype(BF16)
        bh = b_g * w_rem
        kh = k_g * w_rem
        vb = v_g.astype(BF16)
        tasks += [chunk_pair(g, p, at, rt, bt, kt, bh, kh, v_g, vb, gl) for p in range(n_pairs)]
    ys = _round_robin(tasks)
    y_rows = [jnp.concatenate(ys[g * n_pairs:(g + 1) * n_pairs], axis=1) for g in range(g_blk)]

    y = jnp.concatenate(y_rows, axis=0) if g_blk > 1 else y_rows[0]
    inv_n = 1.0 / RW_HEAD
    mean = _seg_sum(y, ones_bd) * inv_n
    yc = y - mean
    var = _seg_sum(yc * yc, ones_bd) * inv_n
    o = yc * lax.rsqrt(var + GN_EPS) * lng_ref[...] + lnb_ref[...]
    out = ((o + bonus) * gate).astype(o_ref.dtype)
    for g in range(g_blk):
        o_ref[g] = out[g * t_blk:(g + 1) * t_blk]

    @pl.when(c == pl.num_programs(1) - 1)
    def _():
        for g in range(g_blk):
            for p in range(n_pairs):
                hm = jnp.where(same_head, h_sc[g, p], 0.0)
                kv = hm[:RW_HEAD] + hm[RW_HEAD:]
                pair = transpose_exact(kv, eye)
                sout_ref[g, 2 * p:2 * p + 2] = pair.reshape(2, RW_HEAD, RW_HEAD)


def _rwkv(zm, zl, prev_m, prev_l, s0, prm, *, batch, seq, g_blk, t_blk, chunk):
    n, w_main = zm.shape
    d_rw = w_main // 3
    n_lora = zl.shape[1]
    n_pairs = d_rw // RW_PAIR
    row = lambda w: pl.BlockSpec((g_blk, t_blk, w), lambda b, c: (b, c, 0))
    per_b = lambda w: pl.BlockSpec((g_blk, 1, w), lambda b, c: (b, 0, 0))
    st_spec = pl.BlockSpec((g_blk, 2 * n_pairs, RW_HEAD, RW_HEAD), lambda b, c: (b, 0, 0, 0))
    vecs = [prm[name] for name in ("mu_m", "mu_l", "w0", "a0", "k_k", "k_a", "r_k", "ln_g", "ln_b")]
    mats = [prm["ww"], prm["wa"], prm["wg"], prm["ones_seg"]]
    o_b, s_new = pl.pallas_call(
        functools.partial(_rwkv_kernel, g_blk=g_blk, t_blk=t_blk, chunk=chunk, d_rw=d_rw),
        grid=(batch // g_blk, seq // t_blk),
        in_specs=[row(w_main), row(n_lora), per_b(w_main), per_b(n_lora), st_spec]
        + [_const_spec(x.shape) for x in vecs + mats],
        out_specs=[row(d_rw), st_spec],
        out_shape=[jax.ShapeDtypeStruct((batch, seq, d_rw), F32),
                   jax.ShapeDtypeStruct(s0.shape, F32)],
        scratch_shapes=[pltpu.VMEM((g_blk, 1, w_main), F32), pltpu.VMEM((g_blk, 1, n_lora), F32),
                        pltpu.VMEM((g_blk, n_pairs, RW_PAIR, RW_PAIR), F32)],
        compiler_params=_params("arbitrary", "arbitrary"),
        name="rwkv_mix",
    )(zm.reshape(batch, seq, w_main), zl.reshape(batch, seq, n_lora),
      prev_m.reshape(batch, 1, w_main), prev_l.reshape(batch, 1, n_lora), s0, *vecs, *mats)
    return o_b.reshape(n, d_rw), s_new


def _ffn_kernel(x_ref, oa_ref, ob_ref, cp_ref, woa_ref, wob_ref, nf_ref, wup_ref, cw_ref,
                cb_ref, wdn_ref, nfin_ref, y_ref, cs_ref, carry_sc, *, d_ff, cw, carried,
                apply_final):
    t = pl.program_id(1)
    tm = x_ref.shape[0]
    groups = tm // SUBLANES

    if carried:
        @pl.when(t == 0)
        def _():
            carry_sc[...] = jnp.zeros(carry_sc.shape, F32)
            carry_sc[:, SUBLANES - (CONV_W - 1):, :] = cp_ref[...]

    x1 = (x_ref[...] + _dot(oa_ref[...].astype(BF16), woa_ref[...])
          + _dot(ob_ref[...].astype(BF16), wob_ref[...]))
    h = _rmsnorm(x1, nf_ref[...], RMS_EPS).astype(BF16)
    t8 = lax.broadcasted_iota(jnp.int32, (groups, SUBLANES, cw), 1)

    def up_proj(ci):
        return [_dot(h, wup_ref[:, col0:col0 + cw]) for col0 in (ci * cw, d_ff + ci * cw)]

    def conv(up, col0):
        cols = slice(col0, col0 + cw)
        up = up.reshape(groups, SUBLANES, cw)
        if carried:
            prev = carry_sc[:, :, cols]
            if groups > 1:
                prev = jnp.concatenate([prev, up[:-1]], axis=0)
            p6, p7 = prev[:, 6:7, :], prev[:, 7:8, :]
            carry_sc[:, :, cols] = up[groups - 1:]
            cs_ref[:, :, cols] = up[groups - 1:, SUBLANES - (CONV_W - 1):, :]
        else:
            p6, p7 = cp_ref[:, 0:1, cols], cp_ref[:, 1:2, cols]
            cs_ref[:, :, cols] = up[:, SUBLANES - (CONV_W - 1):, :]
        m1 = jnp.where(t8 == 0, p7, pltpu.roll(up, 1, axis=1))
        m2 = jnp.where(t8 == 0, p6, jnp.where(t8 == 1, p7, pltpu.roll(up, 2, axis=1)))
        w = cw_ref[:, cols]
        out = cb_ref[:, cols] + m2 * w[0:1] + m1 * w[1:2] + up * w[2:3]
        return out.reshape(tm, cw)

    n_chunks = d_ff // cw
    acc = jnp.zeros((tm, x_ref.shape[1]), F32)
    ups = up_proj(0)
    for ci in range(n_chunks):
        nxt = up_proj(ci + 1) if ci + 1 < n_chunks else None
        gate = conv(ups[0], ci * cw)
        val = conv(ups[1], d_ff + ci * cw)
        act = (gate * _sigmoid(gate) * val).astype(BF16)
        acc = acc + _dot(act, wdn_ref[ci * cw:(ci + 1) * cw, :])
        ups = nxt
    x2 = x1 + acc
    y_ref[...] = _rmsnorm(x2, nfin_ref[...], RMS_EPS) if apply_final else x2


def _ffn(x, oa, ob, conv_prev, woa, wob, nf, wup, cwt, cb, wdn, nfin, *, batch, seq, tm, cw,
         apply_final):
    n, d = x.shape
    d_mix = oa.shape[1]
    d_ff2 = wup.shape[1]
    d_ff = d_ff2 // 2
    carried = seq >= tm
    if carried:
        nt = seq // tm
        grid = (batch, nt)
        row = lambda w: pl.BlockSpec((tm, w), lambda b, t: (b * nt + t, 0))
        cs_spec = pl.BlockSpec((1, CONV_W - 1, d_ff2), lambda b, t: (b, 0, 0))
        cp_spec = cs_spec
    else:
        seqs = tm // seq
        grid = (batch // seqs, 1)
        row = lambda w: pl.BlockSpec((tm, w), lambda b, t: (b, 0))
        cs_spec = pl.BlockSpec((seqs, CONV_W - 1, d_ff2), lambda b, t: (b, 0, 0))
        cp_spec = cs_spec
    return pl.pallas_call(
        functools.partial(_ffn_kernel, d_ff=d_ff, cw=cw, carried=carried,
                          apply_final=apply_final),
        grid=grid,
        in_specs=[row(d), row(d_mix), row(d_mix), cp_spec, _const_spec(woa.shape),
                  _const_spec(wob.shape), _const_spec(nf.shape), _const_spec(wup.shape),
                  _const_spec(cwt.shape), _const_spec(cb.shape), _const_spec(wdn.shape),
                  _const_spec(nfin.shape)],
        out_specs=[row(d), cs_spec],
        out_shape=[jax.ShapeDtypeStruct((n, d), F32),
                   jax.ShapeDtypeStruct((batch, CONV_W - 1, d_ff2), F32)],
        scratch_shapes=[pltpu.VMEM((1, SUBLANES, d_ff2), F32)],
        compiler_params=_params("arbitrary", "arbitrary"),
        name="out_proj_ffn",
    )(x, oa, ob, conv_prev, woa, wob, nf, wup, cwt, cb, wdn, nfin)


def _pick_tile(n, target):
    t = min(n, target)
    while n % t:
        t //= 2
    return t


def kernel(x_prompt, x_sample, cache_k, cache_v, state_wkv, state_shift, state_conv, page_table, norm_mix, w_in, lam_q1, lam_k1, lam_q2, lam_k2, subln, rw_mu, rw_w0, rw_w_up, rw_a0, rw_a_up, rw_g_up, rw_k_k, rw_k_a, rw_r_k, rw_ln_g, rw_ln_b, w_out, norm_ffn, w_up, conv_w, conv_b, w_down, norm_final):
    depth = w_in.shape[0]
    bp, tp, d = x_prompt.shape
    bs, ts, _ = x_sample.shape
    d_rw = rw_w0.shape[1]
    d_diff = DA_HEADS * DA_V
    n_wl, n_al, n_gl = rw_w_up.shape[1], rw_a_up.shape[1], rw_g_up.shape[1]
    n_lora = n_wl + n_al + n_gl
    n_rw_cols = 3 * d_rw + n_lora

    xp = x_prompt.reshape(bp * tp, d)
    xs = x_sample.reshape(bs * ts, d)
    assert (n_wl + n_al) % LANES == 0 and d_rw % (2 * RW_PAIR) == 0
    head_of = jnp.arange(2 * RW_PAIR) // RW_HEAD
    ones_seg = (head_of[:, None] == head_of[None, :]).astype(BF16)
    row2 = lambda vec: vec.reshape(1, -1)

    outs_p = [[] for _ in range(5)]
    outs_s = [[] for _ in range(5)]
    for layer in range(depth):
        lam_init = 0.8 - 0.6 * math.exp(-0.3 * layer)
        last = layer == depth - 1
        wl = w_in[layer]
        wa = wl[:, :3 * d_diff].astype(BF16)
        wr = wl[:, 3 * d_diff:3 * d_diff + 3 * d_rw].astype(BF16)
        wlo = wl[:, 3 * d_diff + 3 * d_rw:].astype(BF16)
        lamv = jnp.stack([lam_q1[layer], lam_k1[layer], lam_q2[layer], lam_k2[layer]])
        sub = row2(subln[layer])
        mu = rw_mu[layer]
        zpad = lambda rows: jnp.zeros((rows, d_rw), BF16)
        prm = {
            "mu_m": row2(mu[:3 * d_rw]), "mu_l": row2(mu[3 * d_rw:]),
            "w0": row2(rw_w0[layer]), "a0": row2(rw_a0[layer]),
            "k_k": row2(rw_k_k[layer]), "k_a": row2(rw_k_a[layer]),
            "r_k": row2(rw_r_k[layer]), "ln_g": row2(rw_ln_g[layer]), "ln_b": row2(rw_ln_b[layer]),
            "ww": jnp.concatenate([rw_w_up[layer].astype(BF16), zpad(n_al)], axis=0),
            "wa": jnp.concatenate([zpad(n_wl), rw_a_up[layer].astype(BF16)], axis=0),
            "wg": rw_g_up[layer].astype(BF16),
            "ones_seg": ones_seg,
        }
        wo = w_out[layer].astype(BF16)
        ffn_w = (wo[:d_diff], wo[d_diff:], row2(norm_ffn[layer]), w_up[layer].astype(BF16),
                 conv_w[layer], row2(conv_b[layer]), w_down[layer].astype(BF16), row2(norm_final))

        def run(x, batch, seq, prev_shift, s0, conv_prev, attn_fn):
            n = batch * seq
            q, k, v, kb, vb, zm, zl = _norm_proj(x, row2(norm_mix[layer]), wa, wr, wlo,
                                                 tm=_pick_tile(n, 256))
            o_a = attn_fn(q, k, v, kb, vb)
            t_blk = min(seq, 64)
            o_b, s_new = _rwkv(zm, zl, prev_shift[:, :3 * d_rw], prev_shift[:, 3 * d_rw:],
                               s0, prm, batch=batch, seq=seq,
                               g_blk=_pick_tile(batch, 4 if t_blk == 64 else 8),
                               t_blk=t_blk, chunk=max(t_blk, 16))
            tm = _pick_tile(n, 512 if seq >= 512 else 256)
            y, conv_new = _ffn(x, o_a, o_b, conv_prev, *ffn_w, batch=batch, seq=seq, tm=tm,
                               cw=256, apply_final=last)
            shift_new = jnp.concatenate([zm.reshape(batch, seq, -1)[:, -1],
                                         zl.reshape(batch, seq, -1)[:, -1]], axis=-1)
            return (y, k.reshape(batch, seq, DA_HEADS, DA_V), v.reshape(batch, seq, DA_HEADS, DA_V),
                    s_new, shift_new, conv_new)

        prompt_attn = lambda q, k, v, kb, vb: _attn_prompt(
            lamv, sub, q, kb, vb, batch=bp, seq=tp, tq=_pick_tile(tp, 256), lam_init=lam_init)
        xp, kp, vp, wp, sp, cp = run(
            xp, bp, tp, jnp.zeros((bp, n_rw_cols), F32),
            jnp.zeros((bp, d_rw // RW_HEAD, RW_HEAD, RW_HEAD), F32),
            jnp.zeros((bp, CONV_W - 1, w_up.shape[2]), F32), prompt_attn)
        sample_attn = lambda q, k, v, kb, vb: _attn_paged(
            page_table, lamv, sub, q, k, v, cache_k, cache_v,
            pages_per_step=_pick_tile(page_table.shape[1], 32),
            pool_offset=layer * cache_k.shape[1], lam_init=lam_init)
        xs, ks, vs, ws, ss, cs = run(
            xs, bs, ts, state_shift[layer], state_wkv[layer],
            state_conv[layer], sample_attn)
        for acc, val in zip(outs_p, (kp, vp, wp, sp, cp)):
            acc.append(val)
        for acc, val in zip(outs_s, (ks, vs, ws, ss, cs)):
            acc.append(val)

    y_prompt = xp.reshape(bp, tp, d)
    y_sample = xs.reshape(bs, ts, d)
    return (y_prompt, y_sample, *[jnp.stack(o) for o in outs_p], *[jnp.stack(o) for o in outs_s])
```

```python
import functools
import math

import jax
import jax.numpy as jnp
from jax import lax
from jax.experimental import pallas as pl
from jax.experimental.pallas import tpu as pltpu

F32 = jnp.float32
BF16 = jnp.bfloat16

LANES = 128
SUBLANES = 8
VMEM_LIMIT_BYTES = 56 * 1024 * 1024

RMS_EPS = 1e-6
SUBLN_EPS = 1e-5
GN_EPS = 64e-5
NEG = -1e30

DA_HEADS = 4
DA_V = 128
DA_QK = 64
RW_HEAD = 64
RW_PAIR = 2 * RW_HEAD
CONV_W = 3


def _params(*sem):
    return pltpu.CompilerParams(dimension_semantics=sem, vmem_limit_bytes=VMEM_LIMIT_BYTES)


def _const_spec(shape):
    zeros = (0,) * len(shape)
    return pl.BlockSpec(shape, lambda *_: zeros, pipeline_mode=pl.Buffered(1))


def _dot(a, b):
    return jnp.dot(a, b, preferred_element_type=F32)


def _dot_nt(a, b):
    return lax.dot_general(a, b, (((1,), (1,)), ((), ())), preferred_element_type=F32)


def _split2(x):
    hi = x.astype(BF16)
    lo = (x - hi.astype(F32)).astype(BF16)
    return hi, lo


def _split3(x):
    hi = x.astype(BF16)
    r1 = x - hi.astype(F32)
    mid = r1.astype(BF16)
    lo = (r1 - mid.astype(F32)).astype(BF16)
    return hi, mid, lo


def _rmsnorm(x, g, eps):
    return x * lax.rsqrt(jnp.mean(x * x, axis=-1, keepdims=True) + eps) * g


def _sigmoid(x):
    return 1.0 / (1.0 + jnp.exp(-x))


def _norm_proj_kernel(x_ref, g_ref, wa_ref, wr_ref, wl_ref,
                      q_ref, k_ref, v_ref, kb_ref, vb_ref, zm_ref, zl_ref, *, d_diff):
    h = _rmsnorm(x_ref[...], g_ref[...], RMS_EPS).astype(BF16)
    pa = _dot(h, wa_ref[...])
    q_ref[...] = pa[:, :d_diff] * (DA_QK ** -0.5)
    k = pa[:, d_diff:2 * d_diff]
    v = pa[:, 2 * d_diff:]
    for head in range(d_diff // DA_V):
        k_ref[:, head, :] = k[:, head * DA_V:(head + 1) * DA_V]
        v_ref[:, head, :] = v[:, head * DA_V:(head + 1) * DA_V]
    kb_ref[...] = k.astype(BF16)
    vb_ref[...] = v.astype(BF16)
    zm_ref[...] = _dot(h, wr_ref[...])
    zl_ref[...] = _dot(h, wl_ref[...])


def _norm_proj(x, g, wa, wr, wl, *, tm):
    n, d = x.shape
    d_diff = wa.shape[1] // 3
    n_rkv, n_lora = wr.shape[1], wl.shape[1]
    heads = d_diff // DA_V
    row = lambda w: pl.BlockSpec((tm, w), lambda i: (i, 0))
    by_head = pl.BlockSpec((tm, heads, DA_V), lambda i: (i, 0, 0))
    return pl.pallas_call(
        functools.partial(_norm_proj_kernel, d_diff=d_diff),
        grid=(n // tm,),
        in_specs=[row(d), _const_spec((1, d)), _const_spec(wa.shape), _const_spec(wr.shape),
                  _const_spec(wl.shape)],
        out_specs=[row(d_diff), by_head, by_head, row(d_diff), row(d_diff),
                   row(n_rkv), row(n_lora)],
        out_shape=[jax.ShapeDtypeStruct((n, d_diff), F32),
                   jax.ShapeDtypeStruct((n, heads, DA_V), F32),
                   jax.ShapeDtypeStruct((n, heads, DA_V), F32),
                   jax.ShapeDtypeStruct((n, d_diff), BF16),
                   jax.ShapeDtypeStruct((n, d_diff), BF16),
                   jax.ShapeDtypeStruct((n, n_rkv), F32),
                   jax.ShapeDtypeStruct((n, n_lora), F32)],
        compiler_params=_params("arbitrary"),
        name="norm_proj",
    )(x, g, wa, wr, wl)


def _lam_value(lamv_ref, lam_init):
    lv = lamv_ref[...]
    s1 = jnp.sum(lv[0:1] * lv[1:2], axis=-1, keepdims=True)
    s2 = jnp.sum(lv[2:3] * lv[3:4], axis=-1, keepdims=True)
    return jnp.exp(s1) - jnp.exp(s2) + lam_init


def _stack_maps(q):
    lane = lax.broadcasted_iota(jnp.int32, q.shape, 1)
    zero = jnp.zeros_like(q)
    return jnp.concatenate([jnp.where(lane < DA_QK, q, zero),
                            jnp.where(lane >= DA_QK, q, zero)], axis=0)


def _attn_prompt_kernel(lamv_ref, subln_ref, bias_ref, q_ref, k_ref, v_ref, o_ref,
                        m_sc, l_sc, acc_sc, *, tq, heads, lam_init):
    i = pl.program_id(1)
    q = q_ref[...]
    lanes = [slice(h * DA_V, (h + 1) * DA_V) for h in range(heads)]
    qs = [_stack_maps(q[:, sl]).astype(BF16) for sl in lanes]
    m_sc[...] = jnp.full(m_sc.shape, NEG, F32)
    l_sc[...] = jnp.zeros(l_sc.shape, F32)
    acc_sc[...] = jnp.zeros(acc_sc.shape, F32)

    def update(start, width, bias):
        rows = pl.ds(start, width)
        scores = [_dot_nt(qs[h], k_ref[rows, lanes[h]]) for h in range(heads)]
        for h in range(heads):
            s = scores[h] if bias is None else scores[h] + bias
            blocks = [s[:, c:c + LANES] for c in range(0, width, LANES)]
            m = m_sc[h]
            row_max = jnp.max(functools.reduce(jnp.maximum, blocks), axis=-1, keepdims=True)
            m_new = jnp.maximum(m, jnp.broadcast_to(row_max, m.shape))
            corr = jnp.exp(m - m_new)
            ps = [jnp.exp(blk - m_new) for blk in blocks]
            m_sc[h] = m_new
            l_sc[h] = l_sc[h] * corr + functools.reduce(lambda a, b: a + b, ps)
            p = jnp.concatenate([x.astype(BF16) for x in ps], axis=1)
            acc_sc[h] = acc_sc[h] * corr + _dot(p, v_ref[rows, lanes[h]])

    @pl.loop(0, i // 2)
    def _(j):
        update(pl.multiple_of(j * 2 * tq, 2 * tq), 2 * tq, None)

    @pl.when(i % 2 == 1)
    def _():
        update(pl.multiple_of((i - 1) * tq, tq), tq, None)

    update(pl.multiple_of(i * tq, tq), tq, bias_ref[...])
    lam = _lam_value(lamv_ref, lam_init)
    for h in range(heads):
        o = acc_sc[h] / jnp.sum(l_sc[h], axis=-1, keepdims=True)
        od = o[:tq] - lam * o[tq:]
        o_ref[:, lanes[h]] = (_rmsnorm(od, subln_ref[...], SUBLN_EPS)
                              * (1.0 - lam_init)).astype(o_ref.dtype)


def _attn_prompt(lamv, subln, q, kb, vb, *, batch, seq, tq, lam_init):
    n, d_diff = q.shape
    heads = d_diff // DA_V
    nq = seq // tq
    kb3 = kb.reshape(batch, seq, d_diff)
    vb3 = vb.reshape(batch, seq, d_diff)
    r = jnp.arange(2 * tq)[:, None] % tq
    causal_bias = jnp.where(jnp.arange(tq)[None, :] <= r, 0.0, NEG).astype(F32)
    qspec = pl.BlockSpec((tq, d_diff), lambda b, i: (b * nq + i, 0))
    kvspec = pl.BlockSpec((None, seq, d_diff), lambda b, i: (b, 0, 0))
    return pl.pallas_call(
        functools.partial(_attn_prompt_kernel, tq=tq, heads=heads, lam_init=lam_init),
        grid=(batch, nq),
        in_specs=[_const_spec(lamv.shape), _const_spec(subln.shape),
                  _const_spec(causal_bias.shape), qspec, kvspec, kvspec],
        out_specs=qspec,
        out_shape=jax.ShapeDtypeStruct((n, d_diff), BF16),
        scratch_shapes=[pltpu.VMEM((heads, 2 * tq, LANES), F32),
                        pltpu.VMEM((heads, 2 * tq, LANES), F32),
                        pltpu.VMEM((heads, 2 * tq, DA_V), F32)],
        compiler_params=_params("arbitrary", "arbitrary"),
        name="attn_prompt",
    )(lamv, subln, causal_bias, q, kb3, vb3)


PAGE_SLOTS = 3


def _attn_paged_kernel(pt_ref, lamv_ref, subln_ref, bias_ref, bias_new_ref, q_ref, kn_ref, vn_ref,
                       ck_ref, cv_ref, o_ref, kbuf, vbuf, sem, s_sc, m_sc, l_sc, acc_sc, *,
                       pages_per_step, pool_offset, n_steps, t_new, lam_init):
    pp = pages_per_step
    g = pl.program_id(1)
    steps_per_seq = pl.num_programs(1)
    step = pl.program_id(0) * steps_per_seq + g
    half = kbuf.shape[2] // 2
    q_rows = 2 * t_new * DA_HEADS
    group = 4

    def page_copies(s):
        slot = s % PAGE_SLOTS
        first = (s % steps_per_seq) * pp
        seq = s // steps_per_seq
        copies = []
        for u in range(pp):
            pid = pool_offset + pt_ref[seq, first + u]
            copies.append(pltpu.make_async_copy(ck_ref.at[pid], kbuf.at[slot, u], sem.at[slot]))
            copies.append(pltpu.make_async_copy(cv_ref.at[pid], vbuf.at[slot, u], sem.at[slot]))
        return copies

    @pl.when(step == 0)
    def _():
        for s in range(min(PAGE_SLOTS - 1, n_steps)):
            for cp in page_copies(s):
                cp.start()

    @pl.when(step + (PAGE_SLOTS - 1) < n_steps)
    def _():
        for cp in page_copies(step + (PAGE_SLOTS - 1)):
            cp.start()

    for cp in page_copies(step):
        cp.wait()
    slot = step % PAGE_SLOTS

    @pl.when(g == 0)
    def _():
        m_sc[...] = jnp.full(m_sc.shape, NEG, F32)
        l_sc[...] = jnp.zeros(l_sc.shape, F32)
        acc_sc[...] = jnp.zeros(acc_sc.shape, F32)

    q = q_ref[...]
    q_all = jnp.concatenate(
        [_stack_maps(q[:, h * DA_V:(h + 1) * DA_V]) for h in range(DA_HEADS)], axis=0)
    zq = jnp.zeros_like(q_all)
    wq = jnp.concatenate([jnp.concatenate([q_all, zq], axis=1),
                          jnp.concatenate([zq, q_all], axis=1)], axis=0).astype(BF16)

    def wide(x):
        return jnp.concatenate([x[:half], x[half:]], axis=1).astype(BF16)

    def to_rows(vec):
        return jnp.broadcast_to(vec, (LANES, LANES)).T[:q_rows]

    def online(n_pg, load_k, load_v, bias):
        for u in range(n_pg):
            s_sc[:, u * LANES:(u + 1) * LANES] = _dot_nt(load_k(u), wq) + bias
        for u0 in range(0, n_pg, group):
            pages = range(u0, min(u0 + group, n_pg))
            m_old = m_sc[...]
            mx = functools.reduce(jnp.maximum, [s_sc[:, u * LANES:(u + 1) * LANES] for u in pages])
            cm = jnp.max(mx, axis=0, keepdims=True)
            m_new = jnp.maximum(m_old, jnp.maximum(cm, pltpu.roll(cm, q_rows, axis=1)))
            corr = jnp.exp(m_old - m_new)
            psum = None
            pv = None
            for u in pages:
                p = jnp.exp(s_sc[:, u * LANES:(u + 1) * LANES] - m_new)
                psum = p if psum is None else psum + p
                d = _dot(p.T.astype(BF16), load_v(u))
                pv = d if pv is None else pv + d
            m_sc[...] = m_new
            l_sc[...] = l_sc[...] * corr + jnp.sum(psum, axis=0, keepdims=True)
            acc_sc[...] = acc_sc[...] * to_rows(corr) + pv[:q_rows, :DA_V] + pv[q_rows:, DA_V:]

    online(pp, lambda u: wide(kbuf[slot, u]), lambda u: wide(vbuf[slot, u]), bias_ref[...])

    @pl.when(g == pl.num_programs(1) - 1)
    def _():
        pad = jnp.zeros((half - kn_ref.shape[0], DA_V), F32)
        zero_half = jnp.zeros((half, DA_V), F32)
        kn = wide(jnp.concatenate([kn_ref[...], pad, zero_half], axis=0))
        vn = wide(jnp.concatenate([vn_ref[...], pad, zero_half], axis=0))
        online(1, lambda u: kn, lambda u: vn, bias_new_ref[...])
        lam = _lam_value(lamv_ref, lam_init)
        l_all = l_sc[...]
        o = acc_sc[...] / to_rows(l_all + pltpu.roll(l_all, q_rows, axis=1))
        for h in range(DA_HEADS):
            base = 2 * t_new * h
            od = o[base:base + t_new] - lam * o[base + t_new:base + 2 * t_new]
            o_ref[:, h * DA_V:(h + 1) * DA_V] = (
                _rmsnorm(od, subln_ref[...], SUBLN_EPS) * (1.0 - lam_init))


def _attn_paged(page_table, lamv, subln, q, k_new, v_new, cache_k, cache_v, *,
                pages_per_step, pool_offset, lam_init):
    nb, n_pages = page_table.shape
    n, d_diff = q.shape
    t_new = n // nb
    page, heads = cache_k.shape[-3], cache_k.shape[-2]
    n_rows = page * heads
    q_rows = 2 * t_new * heads
    pp = pages_per_step
    ck = cache_k.reshape(-1, n_rows, DA_V)
    cv = cache_v.reshape(-1, n_rows, DA_V)
    kn = k_new.reshape(n * heads, DA_V)
    vn = v_new.reshape(n * heads, DA_V)
    assert 2 * q_rows == LANES and n_rows % (2 * SUBLANES) == 0 and t_new * heads <= n_rows // 2
    half = n_rows // 2
    r = jnp.arange(half)[:, None]
    c = jnp.arange(LANES)[None, :]
    same_head = (r % heads) == ((c % q_rows) // (2 * t_new))
    bias = jnp.where(same_head, 0.0, NEG).astype(F32)
    visible = same_head & (c < q_rows) & (r < t_new * heads) & (r // heads <= c % t_new)
    bias_new = jnp.where(visible, 0.0, NEG).astype(F32)
    seq_spec = pl.BlockSpec((t_new, d_diff), lambda b, g, pt: (b, 0))
    new_spec = pl.BlockSpec((t_new * heads, DA_V), lambda b, g, pt: (b, 0))

    const = lambda shape: pl.BlockSpec(shape, lambda b, g, pt: (0,) * len(shape))
    in_hbm = pl.BlockSpec(memory_space=pl.ANY)
    steps_per_seq = n_pages // pp
    grid_spec = pltpu.PrefetchScalarGridSpec(
        num_scalar_prefetch=1,
        grid=(nb, steps_per_seq),
        in_specs=[const(lamv.shape), const(subln.shape), const(bias.shape), const(bias_new.shape),
                  seq_spec, new_spec, new_spec, in_hbm, in_hbm],
        out_specs=seq_spec,
        scratch_shapes=[pltpu.VMEM((PAGE_SLOTS, pp, n_rows, DA_V), F32),
                        pltpu.VMEM((PAGE_SLOTS, pp, n_rows, DA_V), F32),
                        pltpu.SemaphoreType.DMA((PAGE_SLOTS,)),
                        pltpu.VMEM((half, pp * LANES), F32),
                        pltpu.VMEM((1, LANES), F32),
                        pltpu.VMEM((1, LANES), F32),
                        pltpu.VMEM((q_rows, DA_V), F32)],
    )
    return pl.pallas_call(
        functools.partial(_attn_paged_kernel, pages_per_step=pp, pool_offset=pool_offset,
                          n_steps=nb * steps_per_seq, t_new=t_new, lam_init=lam_init),
        grid_spec=grid_spec,
        out_shape=jax.ShapeDtypeStruct((n, d_diff), F32),
        compiler_params=_params("arbitrary", "arbitrary"),
        name="attn_paged",
    )(page_table, lamv, subln, bias, bias_new, q, kn, vn, ck, cv)


def _seg_sum(x, ones_seg):
    width = ones_seg.shape[0]
    outs = []
    for c0 in range(0, x.shape[1], width):
        hi, lo = _split2(x[:, c0:c0 + width])
        outs.append(_dot(hi, ones_seg) + _dot(lo, ones_seg))
    return jnp.concatenate(outs, axis=1)


def _round_robin(tasks):
    results = [None] * len(tasks)
    active = list(enumerate(tasks))
    while active:
        still = []
        for i, task in active:
            try:
                next(task)
                still.append((i, task))
            except StopIteration as stop:
                results[i] = stop.value
        active = still
    return results


def _tri_inverse(mats, c):
    row = lax.broadcasted_iota(jnp.int32, (c, c), 0)
    col = lax.broadcasted_iota(jnp.int32, (c, c), 1)
    eye = jnp.where(row == col, 1.0, 0.0).astype(F32)
    power = list(mats)
    factors = [[(eye + a).astype(BF16)] for a in mats]
    n = 1
    while 2 * n < c:
        yield
        power = [_dot(pb, pb) for pb in (p.astype(BF16) for p in power)]
        for fs, p in zip(factors, power):
            fs.append((eye + p).astype(BF16))
        n *= 2
    while len(factors[0]) > 1:
        yield
        nxt = []
        for fs in factors:
            prod = [_dot(fs[i], fs[i + 1]).astype(BF16) for i in range(0, len(fs) - 1, 2)]
            nxt.append(prod + ([fs[-1]] if len(fs) % 2 else []))
        factors = nxt
    return [fs[0] for fs in factors]


def _rwkv_kernel(zm_ref, zl_ref, pm_ref, pl_ref, s0_ref, mum_ref, mul_ref, w0_ref, a0_ref,
                 kk_ref, ka_ref, rk_ref, lng_ref, lnb_ref, ww_ref, wa_ref, wg_ref, ones_ref,
                 o_ref, sout_ref, cm_sc, cl_sc, h_sc, *, g_blk, t_blk, chunk, d_rw):
    c = pl.program_id(1)
    n_pairs = d_rw // RW_PAIR
    prow = lax.broadcasted_iota(jnp.int32, (RW_PAIR, RW_PAIR), 0)
    pcol = lax.broadcasted_iota(jnp.int32, (RW_PAIR, RW_PAIR), 1)
    same_head = (prow < RW_HEAD) == (pcol < RW_HEAD)
    eye = jnp.where(prow == pcol, 1.0, 0.0).astype(BF16)
    hrow = lax.broadcasted_iota(jnp.int32, (RW_HEAD, RW_HEAD), 0)
    hcol = lax.broadcasted_iota(jnp.int32, (RW_HEAD, RW_HEAD), 1)
    eye_head = jnp.where(hrow == hcol, 1.0, 0.0).astype(BF16)

    def transpose_exact(x, ident):
        return functools.reduce(lambda a, b: a + b, [_dot_nt(ident, part) for part in _split3(x)])

    @pl.when(c == 0)
    def _():
        cm_sc[...] = pm_ref[...]
        cl_sc[...] = pl_ref[...]
        for g in range(g_blk):
            for p in range(n_pairs):
                pair = s0_ref[g, 2 * p:2 * p + 2].reshape(RW_PAIR, RW_HEAD)
                kv = transpose_exact(pair, eye_head)
                h_sc[g, p] = jnp.where(same_head, jnp.concatenate([kv, kv], axis=0), 0.0)

    def shift_mix(z_ref, carry_ref, mu):
        out = []
        for g in range(g_blk):
            z = z_ref[g]
            rows = lax.broadcasted_iota(jnp.int32, z.shape, 0)
            zs = jnp.where(rows == 0, carry_ref[g], pltpu.roll(z, 1, axis=0))
            carry_ref[g] = z[t_blk - 1:t_blk, :]
            out.append(z + (zs - z) * mu)
        return jnp.concatenate(out, axis=0) if g_blk > 1 else out[0]

    zmix = shift_mix(zm_ref, cm_sc, mum_ref[...])
    zlm = shift_mix(zl_ref, cl_sc, mul_ref[...])
    r, k, v = zmix[:, :d_rw], zmix[:, d_rw:2 * d_rw], zmix[:, 2 * d_rw:]

    ones_bd = ones_ref[...]
    n_wa = ww_ref.shape[0]
    z_wa, z_g = zlm[:, :n_wa], zlm[:, n_wa:]
    u = w0_ref[...] + _dot(jnp.tanh(z_wa).astype(BF16), ww_ref[...])
    softplus = jnp.maximum(-u, 0.0) + jnp.log(1.0 + jnp.exp(-jnp.abs(u)))
    lw = -jnp.exp(-softplus - 0.5)
    a = _sigmoid(a0_ref[...] + _dot(z_wa.astype(BF16), wa_ref[...]))
    gate = _dot(_sigmoid(z_g).astype(BF16), wg_ref[...])
    kk = k * kk_ref[...]
    kk = kk / jnp.maximum(jnp.sqrt(_seg_sum(kk * kk, ones_bd)), 1e-12)
    k2 = k * (1.0 + (a - 1.0) * ka_ref[...])
    av = -kk
    bv = kk * a
    bonus = _seg_sum(r * k2 * rk_ref[...], ones_bd) * v

    row = lax.broadcasted_iota(jnp.int32, (chunk, chunk), 0)
    col = lax.broadcasted_iota(jnp.int32, (chunk, chunk), 1)
    tri = jnp.where(row >= col, 1.0, 0.0).astype(BF16)
    zpad = jnp.zeros((chunk - t_blk, d_rw), F32) if chunk > t_blk else None

    def seq_rows(x, g):
        xg = x[g * t_blk:(g + 1) * t_blk]
        return xg if zpad is None else jnp.concatenate([xg, zpad], axis=0)

    lane = lax.broadcasted_iota(jnp.int32, (1, RW_PAIR), 1)
    first = lane < RW_HEAD
    row2 = lax.broadcasted_iota(jnp.int32, (2 * chunk, chunk), 0)
    col2 = lax.broadcasted_iota(jnp.int32, (2 * chunk, chunk), 1)
    mask2 = jnp.where(row2 < chunk, row2, row2 - chunk + 1) > col2
    zero_b = jnp.zeros((), BF16)
    upd_rows = -(-2 * chunk // RW_PAIR) * RW_PAIR
    upd_pad = jnp.zeros((upd_rows - 2 * chunk, RW_PAIR), F32)

    def chunk_pair(g, p, at, rt, bt, kt, bh, kh, v_c, vb, gl):
        sl = slice(p * RW_PAIR, (p + 1) * RW_PAIR)
        lhs = jnp.concatenate([at[:, sl], rt[:, sl]], axis=0)
        h_old = h_sc[g, p]
        hh = _dot(lhs, h_old.astype(BF16))
        ab, ak = [], []
        for e in range(2):
            sel = first if e == 0 else jnp.logical_not(first)
            le = jnp.where(sel, lhs, zero_b)
            ab.append(jnp.where(mask2, _dot_nt(le, bt[:, sl]), 0.0))
            ak.append(jnp.where(mask2, _dot_nt(le, kt[:, sl]), 0.0))
        yield
        xk = [_dot(x.astype(BF16), vb[:, sl]) for x in ak]
        xb = [x[chunk:].astype(BF16) for x in ab]
        tinv = yield from _tri_inverse([x[:chunk] for x in ab], chunk)
        xkv = jnp.where(first, xk[0], xk[1])
        rhs = (hh[:chunk] + xkv[:chunk]).astype(BF16)
        yield
        uu = jnp.where(first, _dot(tinv[0], rhs), _dot(tinv[1], rhs))
        ub = uu.astype(BF16)
        yield
        y = hh[chunk:] + xkv[chunk:] + jnp.where(first, _dot(xb[0], ub), _dot(xb[1], ub))
        pieces_l = [bh[:, sl], kh[:, sl]]
        pieces_r = [uu, v_c[:, sl]]
        if upd_rows > 2 * chunk:
            pieces_l.append(upd_pad)
            pieces_r.append(upd_pad)
        lhs_t = jnp.concatenate(pieces_l, axis=0).T.astype(BF16)
        upd = _dot(lhs_t, jnp.concatenate(pieces_r, axis=0).astype(BF16))
        w_all = jnp.exp(jnp.broadcast_to(gl[:, sl], (RW_PAIR, RW_PAIR))).T
        h_sc[g, p] = h_old * w_all + jnp.where(same_head, upd, 0.0)
        return y[:t_blk]

    tasks = []
    for g in range(g_blk):
        lw_g = seq_rows(lw, g)
        gc = functools.reduce(lambda x, y: x + y, [_dot(tri, part) for part in _split3(lw_g)])
        gl = gc[chunk - 1:chunk, :]
        w_inv = jnp.exp(-gc)
        w_rem = jnp.exp(gl - gc)
        r_g, k_g, v_g, a_g, b_g = (seq_rows(x, g) for x in (r, k2, v, av, bv))
        at = (a_g * jnp.exp(gc - lw_g)).astype(BF16)
        rt = (r_g * jnp.exp(gc)).astype(BF16)
        bt = (b_g * w_inv).astype(BF16)
        kt = (k_g * w_inv).astype(BF16)
        bh = b_g * w_rem
        kh = k_g * w_rem
        vb = v_g.astype(BF16)
        tasks += [chunk_pair(g, p, at, rt, bt, kt, bh, kh, v_g, vb, gl) for p in range(n_pairs)]
    ys = _round_robin(tasks)
    y_rows = [jnp.concatenate(ys[g * n_pairs:(g + 1) * n_pairs], axis=1) for g in range(g_blk)]

    y = jnp.concatenate(y_rows, axis=0) if g_blk > 1 else y_rows[0]
    inv_n = 1.0 / RW_HEAD
    mean = _seg_sum(y, ones_bd) * inv_n
    yc = y - mean
    var = _seg_sum(yc * yc, ones_bd) * inv_n
    o = yc * lax.rsqrt(var + GN_EPS) * lng_ref[...] + lnb_ref[...]
    out = ((o + bonus) * gate).astype(o_ref.dtype)
    for g in range(g_blk):
        o_ref[g] = out[g * t_blk:(g + 1) * t_blk]

    @pl.when(c == pl.num_programs(1) - 1)
    def _():
        for g in range(g_blk):
            for p in range(n_pairs):
                hm = jnp.where(same_head, h_sc[g, p], 0.0)
                kv = hm[:RW_HEAD] + hm[RW_HEAD:]
                pair = transpose_exact(kv, eye)
                sout_ref[g, 2 * p:2 * p + 2] = pair.reshape(2, RW_HEAD, RW_HEAD)


def _rwkv(zm, zl, prev_m, prev_l, s0, prm, *, batch, seq, g_blk, t_blk, chunk):
    n, w_main = zm.shape
    d_rw = w_main // 3
    n_lora = zl.shape[1]
    n_pairs = d_rw // RW_PAIR
    row = lambda w: pl.BlockSpec((g_blk, t_blk, w), lambda b, c: (b, c, 0))
    per_b = lambda w: pl.BlockSpec((g_blk, 1, w), lambda b, c: (b, 0, 0))
    st_spec = pl.BlockSpec((g_blk, 2 * n_pairs, RW_HEAD, RW_HEAD), lambda b, c: (b, 0, 0, 0))
    vecs = [prm[name] for name in ("mu_m", "mu_l", "w0", "a0", "k_k", "k_a", "r_k", "ln_g", "ln_b")]
    mats = [prm["ww"], prm["wa"], prm["wg"], prm["ones_seg"]]
    o_b, s_new = pl.pallas_call(
        functools.partial(_rwkv_kernel, g_blk=g_blk, t_blk=t_blk, chunk=chunk, d_rw=d_rw),
        grid=(batch // g_blk, seq // t_blk),
        in_specs=[row(w_main), row(n_lora), per_b(w_main), per_b(n_lora), st_spec]
        + [_const_spec(x.shape) for x in vecs + mats],
        out_specs=[row(d_rw), st_spec],
        out_shape=[jax.ShapeDtypeStruct((batch, seq, d_rw), F32),
                   jax.ShapeDtypeStruct(s0.shape, F32)],
        scratch_shapes=[pltpu.VMEM((g_blk, 1, w_main), F32), pltpu.VMEM((g_blk, 1, n_lora), F32),
                        pltpu.VMEM((g_blk, n_pairs, RW_PAIR, RW_PAIR), F32)],
        compiler_params=_params("arbitrary", "arbitrary"),
        name="rwkv_mix",
    )(zm.reshape(batch, seq, w_main), zl.reshape(batch, seq, n_lora),
      prev_m.reshape(batch, 1, w_main), prev_l.reshape(batch, 1, n_lora), s0, *vecs, *mats)
    return o_b.reshape(n, d_rw), s_new


def _ffn_kernel(x_ref, oa_ref, ob_ref, cp_ref, woa_ref, wob_ref, nf_ref, wup_ref, cw_ref,
                cb_ref, wdn_ref, nfin_ref, y_ref, cs_ref, carry_sc, *, d_ff, cw, carried,
                apply_final):
    t = pl.program_id(1)
    tm = x_ref.shape[0]
    groups = tm // SUBLANES

    if carried:
        @pl.when(t == 0)
        def _():
            carry_sc[...] = jnp.zeros(carry_sc.shape, F32)
            carry_sc[:, SUBLANES - (CONV_W - 1):, :] = cp_ref[...]

    x1 = (x_ref[...] + _dot(oa_ref[...].astype(BF16), woa_ref[...])
          + _dot(ob_ref[...].astype(BF16), wob_ref[...]))
    h = _rmsnorm(x1, nf_ref[...], RMS_EPS).astype(BF16)
    t8 = lax.broadcasted_iota(jnp.int32, (groups, SUBLANES, cw), 1)

    def up_proj(ci):
        return [_dot(h, wup_ref[:, col0:col0 + cw]) for col0 in (ci * cw, d_ff + ci * cw)]

    def conv(up, col0):
        cols = slice(col0, col0 + cw)
        up = up.reshape(groups, SUBLANES, cw)
        if carried:
            prev = carry_sc[:, :, cols]
            if groups > 1:
                prev = jnp.concatenate([prev, up[:-1]], axis=0)
            p6, p7 = prev[:, 6:7, :], prev[:, 7:8, :]
            carry_sc[:, :, cols] = up[groups - 1:]
            cs_ref[:, :, cols] = up[groups - 1:, SUBLANES - (CONV_W - 1):, :]
        else:
            p6, p7 = cp_ref[:, 0:1, cols], cp_ref[:, 1:2, cols]
            cs_ref[:, :, cols] = up[:, SUBLANES - (CONV_W - 1):, :]
        m1 = jnp.where(t8 == 0, p7, pltpu.roll(up, 1, axis=1))
        m2 = jnp.where(t8 == 0, p6, jnp.where(t8 == 1, p7, pltpu.roll(up, 2, axis=1)))
        w = cw_ref[:, cols]
        out = cb_ref[:, cols] + m2 * w[0:1] + m1 * w[1:2] + up * w[2:3]
        return out.reshape(tm, cw)

    n_chunks = d_ff // cw
    acc = jnp.zeros((tm, x_ref.shape[1]), F32)
    ups = up_proj(0)
    for ci in range(n_chunks):
        nxt = up_proj(ci + 1) if ci + 1 < n_chunks else None
        gate = conv(ups[0], ci * cw)
        val = conv(ups[1], d_ff + ci * cw)
        act = (gate * _sigmoid(gate) * val).astype(BF16)
        acc = acc + _dot(act, wdn_ref[ci * cw:(ci + 1) * cw, :])
        ups = nxt
    x2 = x1 + acc
    y_ref[...] = _rmsnorm(x2, nfin_ref[...], RMS_EPS) if apply_final else x2


def _ffn(x, oa, ob, conv_prev, woa, wob, nf, wup, cwt, cb, wdn, nfin, *, batch, seq, tm, cw,
         apply_final):
    n, d = x.shape
    d_mix = oa.shape[1]
    d_ff2 = wup.shape[1]
    d_ff = d_ff2 // 2
    carried = seq >= tm
    if carried:
        nt = seq // tm
        grid = (batch, nt)
        row = lambda w: pl.BlockSpec((tm, w), lambda b, t: (b * nt + t, 0))
        cs_spec = pl.BlockSpec((1, CONV_W - 1, d_ff2), lambda b, t: (b, 0, 0))
        cp_spec = cs_spec
    else:
        seqs = tm // seq
        grid = (batch // seqs, 1)
        row = lambda w: pl.BlockSpec((tm, w), lambda b, t: (b, 0))
        cs_spec = pl.BlockSpec((seqs, CONV_W - 1, d_ff2), lambda b, t: (b, 0, 0))
        cp_spec = cs_spec
    return pl.pallas_call(
        functools.partial(_ffn_kernel, d_ff=d_ff, cw=cw, carried=carried,
                          apply_final=apply_final),
        grid=grid,
        in_specs=[row(d), row(d_mix), row(d_mix), cp_spec, _const_spec(woa.shape),
                  _const_spec(wob.shape), _const_spec(nf.shape), _const_spec(wup.shape),
                  _const_spec(cwt.shape), _const_spec(cb.shape), _const_spec(wdn.shape),
                  _const_spec(nfin.shape)],
        out_specs=[row(d), cs_spec],
        out_shape=[jax.ShapeDtypeStruct((n, d), F32),
                   jax.ShapeDtypeStruct((batch, CONV_W - 1, d_ff2), F32)],
        scratch_shapes=[pltpu.VMEM((1, SUBLANES, d_ff2), F32)],
        compiler_params=_params("arbitrary", "arbitrary"),
        name="out_proj_ffn",
    )(x, oa, ob, conv_prev, woa, wob, nf, wup, cwt, cb, wdn, nfin)


def _pick_tile(n, target):
    t = min(n, target)
    while n % t:
        t //= 2
    return t


def kernel(x_prompt, x_sample, cache_k, cache_v, state_wkv, state_shift, state_conv, page_table, norm_mix, w_in, lam_q1, lam_k1, lam_q2, lam_k2, subln, rw_mu, rw_w0, rw_w_up, rw_a0, rw_a_up, rw_g_up, rw_k_k, rw_k_a, rw_r_k, rw_ln_g, rw_ln_b, w_out, norm_ffn, w_up, conv_w, conv_b, w_down, norm_final):
    depth = w_in.shape[0]
    bp, tp, d = x_prompt.shape
    bs, ts, _ = x_sample.shape
    d_rw = rw_w0.shape[1]
    d_diff = DA_HEADS * DA_V
    n_wl, n_al, n_gl = rw_w_up.shape[1], rw_a_up.shape[1], rw_g_up.shape[1]
    n_lora = n_wl + n_al + n_gl
    n_rw_cols = 3 * d_rw + n_lora

    xp = x_prompt.reshape(bp * tp, d)
    xs = x_sample.reshape(bs * ts, d)
    assert (n_wl + n_al) % LANES == 0 and d_rw % (2 * RW_PAIR) == 0
    head_of = jnp.arange(2 * RW_PAIR) // RW_HEAD
    ones_seg = (head_of[:, None] == head_of[None, :]).astype(BF16)
    row2 = lambda vec: vec.reshape(1, -1)

    outs_p = [[] for _ in range(5)]
    outs_s = [[] for _ in range(5)]
    for layer in range(depth):
        lam_init = 0.8 - 0.6 * math.exp(-0.3 * layer)
        last = layer == depth - 1
        wl = w_in[layer]
        wa = wl[:, :3 * d_diff].astype(BF16)
        wr = wl[:, 3 * d_diff:3 * d_diff + 3 * d_rw].astype(BF16)
        wlo = wl[:, 3 * d_diff + 3 * d_rw:].astype(BF16)
        lamv = jnp.stack([lam_q1[layer], lam_k1[layer], lam_q2[layer], lam_k2[layer]])
        sub = row2(subln[layer])
        mu = rw_mu[layer]
        zpad = lambda rows: jnp.zeros((rows, d_rw), BF16)
        prm = {
            "mu_m": row2(mu[:3 * d_rw]), "mu_l": row2(mu[3 * d_rw:]),
            "w0": row2(rw_w0[layer]), "a0": row2(rw_a0[layer]),
            "k_k": row2(rw_k_k[layer]), "k_a": row2(rw_k_a[layer]),
            "r_k": row2(rw_r_k[layer]), "ln_g": row2(rw_ln_g[layer]), "ln_b": row2(rw_ln_b[layer]),
            "ww": jnp.concatenate([rw_w_up[layer].astype(BF16), zpad(n_al)], axis=0),
            "wa": jnp.concatenate([zpad(n_wl), rw_a_up[layer].astype(BF16)], axis=0),
            "wg": rw_g_up[layer].astype(BF16),
            "ones_seg": ones_seg,
        }
        wo = w_out[layer].astype(BF16)
        ffn_w = (wo[:d_diff], wo[d_diff:], row2(norm_ffn[layer]), w_up[layer].astype(BF16),
                 conv_w[layer], row2(conv_b[layer]), w_down[layer].astype(BF16), row2(norm_final))

        def run(x, batch, seq, prev_shift, s0, conv_prev, attn_fn):
            n = batch * seq
            q, k, v, kb, vb, zm, zl = _norm_proj(x, row2(norm_mix[layer]), wa, wr, wlo,
                                                 tm=_pick_tile(n, 256))
            o_a = attn_fn(q, k, v, kb, vb)
            t_blk = min(seq, 64)
            o_b, s_new = _rwkv(zm, zl, prev_shift[:, :3 * d_rw], prev_shift[:, 3 * d_rw:],
                               s0, prm, batch=batch, seq=seq,
                               g_blk=_pick_tile(batch, 4 if t_blk == 64 else 8),
                               t_blk=t_blk, chunk=max(t_blk, 16))
            tm = _pick_tile(n, 512 if seq >= 512 else 256)
            y, conv_new = _ffn(x, o_a, o_b, conv_prev, *ffn_w, batch=batch, seq=seq, tm=tm,
                               cw=256, apply_final=last)
            shift_new = jnp.concatenate([zm.reshape(batch, seq, -1)[:, -1],
                                         zl.reshape(batch, seq, -1)[:, -1]], axis=-1)
            return (y, k.reshape(batch, seq, DA_HEADS, DA_V), v.reshape(batch, seq, DA_HEADS, DA_V),
                    s_new, shift_new, conv_new)

        prompt_attn = lambda q, k, v, kb, vb: _attn_prompt(
            lamv, sub, q, kb, vb, batch=bp, seq=tp, tq=_pick_tile(tp, 256), lam_init=lam_init)
        xp, kp, vp, wp, sp, cp = run(
            xp, bp, tp, jnp.zeros((bp, n_rw_cols), F32),
            jnp.zeros((bp, d_rw // RW_HEAD, RW_HEAD, RW_HEAD), F32),
            jnp.zeros((bp, CONV_W - 1, w_up.shape[2]), F32), prompt_attn)
        sample_attn = lambda q, k, v, kb, vb: _attn_paged(
            page_table, lamv, sub, q, k, v, cache_k, cache_v,
            pages_per_step=_pick_tile(page_table.shape[1], 16),
            pool_offset=layer * cache_k.shape[1], lam_init=lam_init)
        xs, ks, vs, ws, ss, cs = run(
            xs, bs, ts, state_shift[layer], state_wkv[layer],
            state_conv[layer], sample_attn)
        for acc, val in zip(outs_p, (kp, vp, wp, sp, cp)):
            acc.append(val)
        for acc, val in zip(outs_s, (ks, vs, ws, ss, cs)):
            acc.append(val)

    y_prompt = xp.reshape(bp, tp, d)
    y_sample = xs.reshape(bs, ts, d)
    return (y_prompt, y_sample, *[jnp.stack(o) for o in outs_p], *[jnp.stack(o) for o in outs_s])
```

```python
import functools
import math

import jax
import jax.numpy as jnp
from jax import lax
from jax.experimental import pallas as pl
from jax.experimental.pallas import tpu as pltpu

F32 = jnp.float32
BF16 = jnp.bfloat16

LANES = 128
SUBLANES = 8
VMEM_LIMIT_BYTES = 56 * 1024 * 1024

RMS_EPS = 1e-6
SUBLN_EPS = 1e-5
GN_EPS = 64e-5
NEG = -1e30

DA_HEADS = 4
DA_V = 128
DA_QK = 64
RW_HEAD = 64
RW_PAIR = 2 * RW_HEAD
CONV_W = 3


def _params(*sem):
    return pltpu.CompilerParams(dimension_semantics=sem, vmem_limit_bytes=VMEM_LIMIT_BYTES)


def _const_spec(shape):
    zeros = (0,) * len(shape)
    return pl.BlockSpec(shape, lambda *_: zeros, pipeline_mode=pl.Buffered(1))


def _dot(a, b):
    return jnp.dot(a, b, preferred_element_type=F32)


def _dot_nt(a, b):
    return lax.dot_general(a, b, (((1,), (1,)), ((), ())), preferred_element_type=F32)


def _split2(x):
    hi = x.astype(BF16)
    lo = (x - hi.astype(F32)).astype(BF16)
    return hi, lo


def _split3(x):
    hi = x.astype(BF16)
    r1 = x - hi.astype(F32)
    mid = r1.astype(BF16)
    lo = (r1 - mid.astype(F32)).astype(BF16)
    return hi, mid, lo


def _rmsnorm(x, g, eps):
    return x * lax.rsqrt(jnp.mean(x * x, axis=-1, keepdims=True) + eps) * g


def _sigmoid(x):
    return 1.0 / (1.0 + jnp.exp(-x))


def _norm_proj_kernel(x_ref, g_ref, w_ref,
                      q_ref, k_ref, v_ref, kb_ref, vb_ref, zm_ref, zl_ref, *, d_diff, n_rkv):
    h = _rmsnorm(x_ref[...], g_ref[...], RMS_EPS).astype(BF16)
    pa = _dot(h, w_ref[...])
    q_ref[...] = (pa[:, :d_diff] * (DA_QK ** -0.5)).astype(q_ref.dtype)
    k = pa[:, d_diff:2 * d_diff]
    v = pa[:, 2 * d_diff:3 * d_diff]
    for head in range(d_diff // DA_V):
        k_ref[:, head, :] = k[:, head * DA_V:(head + 1) * DA_V]
        v_ref[:, head, :] = v[:, head * DA_V:(head + 1) * DA_V]
    kb_ref[...] = k.astype(BF16)
    vb_ref[...] = v.astype(BF16)
    zm_ref[...] = pa[:, 3 * d_diff:3 * d_diff + n_rkv]
    zl_ref[...] = pa[:, 3 * d_diff + n_rkv:]


def _norm_proj(x, g, w, *, d_diff, n_rkv, tm, q_dtype):
    n, d = x.shape
    n_lora = w.shape[1] - 3 * d_diff - n_rkv
    heads = d_diff // DA_V
    row = lambda w: pl.BlockSpec((tm, w), lambda i: (i, 0))
    by_head = pl.BlockSpec((tm, heads, DA_V), lambda i: (i, 0, 0))
    return pl.pallas_call(
        functools.partial(_norm_proj_kernel, d_diff=d_diff, n_rkv=n_rkv),
        grid=(n // tm,),
        in_specs=[row(d), _const_spec((1, d)), _const_spec(w.shape)],
        out_specs=[row(d_diff), by_head, by_head, row(d_diff), row(d_diff),
                   row(n_rkv), row(n_lora)],
        out_shape=[jax.ShapeDtypeStruct((n, d_diff), q_dtype),
                   jax.ShapeDtypeStruct((n, heads, DA_V), F32),
                   jax.ShapeDtypeStruct((n, heads, DA_V), F32),
                   jax.ShapeDtypeStruct((n, d_diff), BF16),
                   jax.ShapeDtypeStruct((n, d_diff), BF16),
                   jax.ShapeDtypeStruct((n, n_rkv), F32),
                   jax.ShapeDtypeStruct((n, n_lora), F32)],
        compiler_params=_params("arbitrary"),
        name="norm_proj",
    )(x, g, w)


def _lam_value(lamv_ref, lam_init):
    lv = lamv_ref[...]
    s1 = jnp.sum(lv[0:1] * lv[1:2], axis=-1, keepdims=True)
    s2 = jnp.sum(lv[2:3] * lv[3:4], axis=-1, keepdims=True)
    return jnp.exp(s1) - jnp.exp(s2) + lam_init


def _stack_maps(q):
    lane = lax.broadcasted_iota(jnp.int32, q.shape, 1)
    zero = jnp.zeros_like(q)
    return jnp.concatenate([jnp.where(lane < DA_QK, q, zero),
                            jnp.where(lane >= DA_QK, q, zero)], axis=0)


def _attn_prompt_kernel(lamv_ref, subln_ref, bias_ref, q_ref, k_ref, v_ref, o_ref,
                        m_sc, l_sc, acc_sc, *, tq, heads, lam_init):
    i = pl.program_id(1)
    q = q_ref[...]
    lanes = [slice(h * DA_V, (h + 1) * DA_V) for h in range(heads)]
    qs = [_stack_maps(q[:, sl]).astype(BF16) for sl in lanes]
    m_sc[...] = jnp.full(m_sc.shape, NEG, F32)
    l_sc[...] = jnp.zeros(l_sc.shape, F32)
    acc_sc[...] = jnp.zeros(acc_sc.shape, F32)

    def update(start, width, bias):
        rows = pl.ds(start, width)
        scores = [_dot_nt(qs[h], k_ref[rows, lanes[h]]) for h in range(heads)]
        for h in range(heads):
            s = scores[h] if bias is None else scores[h] + bias
            blocks = [s[:, c:c + LANES] for c in range(0, width, LANES)]
            m = m_sc[h]
            row_max = jnp.max(functools.reduce(jnp.maximum, blocks), axis=-1, keepdims=True)
            m_new = jnp.maximum(m, jnp.broadcast_to(row_max, m.shape))
            corr = jnp.exp(m - m_new)
            ps = [jnp.exp(blk - m_new) for blk in blocks]
            m_sc[h] = m_new
            l_sc[h] = l_sc[h] * corr + functools.reduce(lambda a, b: a + b, ps)
            p = jnp.concatenate([x.astype(BF16) for x in ps], axis=1)
            acc_sc[h] = acc_sc[h] * corr + _dot(p, v_ref[rows, lanes[h]])

    @pl.loop(0, i // 2)
    def _(j):
        update(pl.multiple_of(j * 2 * tq, 2 * tq), 2 * tq, None)

    @pl.when(i % 2 == 1)
    def _():
        update(pl.multiple_of((i - 1) * tq, tq), tq, None)

    update(pl.multiple_of(i * tq, tq), tq, bias_ref[...])
    lam = _lam_value(lamv_ref, lam_init)
    for h in range(heads):
        o = acc_sc[h] / jnp.sum(l_sc[h], axis=-1, keepdims=True)
        od = o[:tq] - lam * o[tq:]
        o_ref[:, lanes[h]] = (_rmsnorm(od, subln_ref[...], SUBLN_EPS)
                              * (1.0 - lam_init)).astype(o_ref.dtype)


def _attn_prompt(lamv, subln, q, kb, vb, *, batch, seq, tq, lam_init):
    n, d_diff = q.shape
    heads = d_diff // DA_V
    nq = seq // tq
    kb3 = kb.reshape(batch, seq, d_diff)
    vb3 = vb.reshape(batch, seq, d_diff)
    r = jnp.arange(2 * tq)[:, None] % tq
    causal_bias = jnp.where(jnp.arange(tq)[None, :] <= r, 0.0, NEG).astype(F32)
    qspec = pl.BlockSpec((tq, d_diff), lambda b, i: (b * nq + i, 0))
    kvspec = pl.BlockSpec((None, seq, d_diff), lambda b, i: (b, 0, 0))
    return pl.pallas_call(
        functools.partial(_attn_prompt_kernel, tq=tq, heads=heads, lam_init=lam_init),
        grid=(batch, nq),
        in_specs=[_const_spec(lamv.shape), _const_spec(subln.shape),
                  _const_spec(causal_bias.shape), qspec, kvspec, kvspec],
        out_specs=qspec,
        out_shape=jax.ShapeDtypeStruct((n, d_diff), BF16),
        scratch_shapes=[pltpu.VMEM((heads, 2 * tq, LANES), F32),
                        pltpu.VMEM((heads, 2 * tq, LANES), F32),
                        pltpu.VMEM((heads, 2 * tq, DA_V), F32)],
        compiler_params=_params("arbitrary", "arbitrary"),
        name="attn_prompt",
    )(lamv, subln, causal_bias, q, kb3, vb3)


PAGE_SLOTS = 3


def _attn_paged_kernel(pt_ref, lamv_ref, subln_ref, bias_ref, bias_new_ref, q_ref, kn_ref, vn_ref,
                       ck_ref, cv_ref, o_ref, kbuf, vbuf, sem, s_sc, m_sc, l_sc, acc_sc, *,
                       pages_per_step, pool_offset, n_steps, t_new, lam_init):
    pp = pages_per_step
    g = pl.program_id(1)
    steps_per_seq = pl.num_programs(1)
    step = pl.program_id(0) * steps_per_seq + g
    half = kbuf.shape[2] // 2
    q_rows = 2 * t_new * DA_HEADS
    group = 4

    def page_copies(s):
        slot = s % PAGE_SLOTS
        first = (s % steps_per_seq) * pp
        seq = s // steps_per_seq
        copies = []
        for u in range(pp):
            pid = pool_offset + pt_ref[seq, first + u]
            copies.append(pltpu.make_async_copy(ck_ref.at[pid], kbuf.at[slot, u], sem.at[slot]))
            copies.append(pltpu.make_async_copy(cv_ref.at[pid], vbuf.at[slot, u], sem.at[slot]))
        return copies

    @pl.when(step == 0)
    def _():
        for s in range(min(PAGE_SLOTS - 1, n_steps)):
            for cp in page_copies(s):
                cp.start()

    @pl.when(step + (PAGE_SLOTS - 1) < n_steps)
    def _():
        for cp in page_copies(step + (PAGE_SLOTS - 1)):
            cp.start()

    for cp in page_copies(step):
        cp.wait()
    slot = step % PAGE_SLOTS

    @pl.when(g == 0)
    def _():
        m_sc[...] = jnp.full(m_sc.shape, NEG, F32)
        l_sc[...] = jnp.zeros(l_sc.shape, F32)
        acc_sc[...] = jnp.zeros(acc_sc.shape, F32)

    q = q_ref[...]
    q_all = jnp.concatenate(
        [_stack_maps(q[:, h * DA_V:(h + 1) * DA_V]) for h in range(DA_HEADS)], axis=0)
    zq = jnp.zeros_like(q_all)
    wq = jnp.concatenate([jnp.concatenate([q_all, zq], axis=1),
                          jnp.concatenate([zq, q_all], axis=1)], axis=0).astype(BF16)

    def wide(x):
        return jnp.concatenate([x[:half], x[half:]], axis=1).astype(BF16)

    def to_rows(vec):
        return jnp.broadcast_to(vec, (LANES, LANES)).T[:q_rows]

    def online(n_pg, load_k, load_v, bias):
        for u in range(n_pg):
            s_sc[:, u * LANES:(u + 1) * LANES] = _dot_nt(load_k(u), wq) + bias
        for u0 in range(0, n_pg, group):
            pages = range(u0, min(u0 + group, n_pg))
            m_old = m_sc[...]
            mx = functools.reduce(jnp.maximum, [s_sc[:, u * LANES:(u + 1) * LANES] for u in pages])
            cm = jnp.max(mx, axis=0, keepdims=True)
            m_new = jnp.maximum(m_old, jnp.maximum(cm, pltpu.roll(cm, q_rows, axis=1)))
            corr = jnp.exp(m_old - m_new)
            psum = None
            pv = None
            for u in pages:
                p = jnp.exp(s_sc[:, u * LANES:(u + 1) * LANES] - m_new)
                psum = p if psum is None else psum + p
                d = _dot(p.T.astype(BF16), load_v(u))
                pv = d if pv is None else pv + d
            m_sc[...] = m_new
            l_sc[...] = l_sc[...] * corr + jnp.sum(psum, axis=0, keepdims=True)
            acc_sc[...] = acc_sc[...] * to_rows(corr) + pv[:q_rows, :DA_V] + pv[q_rows:, DA_V:]

    online(pp, lambda u: wide(kbuf[slot, u]), lambda u: wide(vbuf[slot, u]), bias_ref[...])

    @pl.when(g == pl.num_programs(1) - 1)
    def _():
        pad = jnp.zeros((half - kn_ref.shape[0], DA_V), F32)
        zero_half = jnp.zeros((half, DA_V), F32)
        kn = wide(jnp.concatenate([kn_ref[...], pad, zero_half], axis=0))
        vn = wide(jnp.concatenate([vn_ref[...], pad, zero_half], axis=0))
        online(1, lambda u: kn, lambda u: vn, bias_new_ref[...])
        lam = _lam_value(lamv_ref, lam_init)
        l_all = l_sc[...]
        o = acc_sc[...] / to_rows(l_all + pltpu.roll(l_all, q_rows, axis=1))
        for h in range(DA_HEADS):
            base = 2 * t_new * h
            od = o[base:base + t_new] - lam * o[base + t_new:base + 2 * t_new]
            o_ref[:, h * DA_V:(h + 1) * DA_V] = (
                _rmsnorm(od, subln_ref[...], SUBLN_EPS) * (1.0 - lam_init))


def _attn_paged(page_table, lamv, subln, q, k_new, v_new, cache_k, cache_v, *,
                pages_per_step, pool_offset, lam_init):
    nb, n_pages = page_table.shape
    n, d_diff = q.shape
    t_new = n // nb
    page, heads = cache_k.shape[-3], cache_k.shape[-2]
    n_rows = page * heads
    q_rows = 2 * t_new * heads
    pp = pages_per_step
    ck = cache_k.reshape(-1, n_rows, DA_V)
    cv = cache_v.reshape(-1, n_rows, DA_V)
    kn = k_new.reshape(n * heads, DA_V)
    vn = v_new.reshape(n * heads, DA_V)
    assert 2 * q_rows == LANES and n_rows % (2 * SUBLANES) == 0 and t_new * heads <= n_rows // 2
    half = n_rows // 2
    r = jnp.arange(half)[:, None]
    c = jnp.arange(LANES)[None, :]
    same_head = (r % heads) == ((c % q_rows) // (2 * t_new))
    bias = jnp.where(same_head, 0.0, NEG).astype(F32)
    visible = same_head & (c < q_rows) & (r < t_new * heads) & (r // heads <= c % t_new)
    bias_new = jnp.where(visible, 0.0, NEG).astype(F32)
    seq_spec = pl.BlockSpec((t_new, d_diff), lambda b, g, pt: (b, 0))
    new_spec = pl.BlockSpec((t_new * heads, DA_V), lambda b, g, pt: (b, 0))

    const = lambda shape: pl.BlockSpec(shape, lambda b, g, pt: (0,) * len(shape))
    in_hbm = pl.BlockSpec(memory_space=pl.ANY)
    steps_per_seq = n_pages // pp
    grid_spec = pltpu.PrefetchScalarGridSpec(
        num_scalar_prefetch=1,
        grid=(nb, steps_per_seq),
        in_specs=[const(lamv.shape), const(subln.shape), const(bias.shape), const(bias_new.shape),
                  seq_spec, new_spec, new_spec, in_hbm, in_hbm],
        out_specs=seq_spec,
        scratch_shapes=[pltpu.VMEM((PAGE_SLOTS, pp, n_rows, DA_V), F32),
                        pltpu.VMEM((PAGE_SLOTS, pp, n_rows, DA_V), F32),
                        pltpu.SemaphoreType.DMA((PAGE_SLOTS,)),
                        pltpu.VMEM((half, pp * LANES), F32),
                        pltpu.VMEM((1, LANES), F32),
                        pltpu.VMEM((1, LANES), F32),
                        pltpu.VMEM((q_rows, DA_V), F32)],
    )
    return pl.pallas_call(
        functools.partial(_attn_paged_kernel, pages_per_step=pp, pool_offset=pool_offset,
                          n_steps=nb * steps_per_seq, t_new=t_new, lam_init=lam_init),
        grid_spec=grid_spec,
        out_shape=jax.ShapeDtypeStruct((n, d_diff), F32),
        compiler_params=_params("arbitrary", "arbitrary"),
        name="attn_paged",
    )(page_table, lamv, subln, bias, bias_new, q, kn, vn, ck, cv)


def _seg_sum(x, ones_seg):
    width = ones_seg.shape[0]
    outs = []
    for c0 in range(0, x.shape[1], width):
        hi, lo = _split2(x[:, c0:c0 + width])
        outs.append(_dot(hi, ones_seg) + _dot(lo, ones_seg))
    return jnp.concatenate(outs, axis=1)


def _round_robin(tasks):
    results = [None] * len(tasks)
    active = list(enumerate(tasks))
    while active:
        still = []
        for i, task in active:
            try:
                next(task)
                still.append((i, task))
            except StopIteration as stop:
                results[i] = stop.value
        active = still
    return results


def _tri_inverse(mats, c):
    row = lax.broadcasted_iota(jnp.int32, (c, c), 0)
    col = lax.broadcasted_iota(jnp.int32, (c, c), 1)
    eye = jnp.where(row == col, 1.0, 0.0).astype(F32)
    power = list(mats)
    factors = [[(eye + a).astype(BF16)] for a in mats]
    n = 1
    while 2 * n < c:
        yield
        power = [_dot(pb, pb) for pb in (p.astype(BF16) for p in power)]
        for fs, p in zip(factors, power):
            fs.append((eye + p).astype(BF16))
        n *= 2
    while len(factors[0]) > 1:
        yield
        nxt = []
        for fs in factors:
            prod = [_dot(fs[i], fs[i + 1]).astype(BF16) for i in range(0, len(fs) - 1, 2)]
            nxt.append(prod + ([fs[-1]] if len(fs) % 2 else []))
        factors = nxt
    return [fs[0] for fs in factors]


def _rwkv_kernel(zm_ref, zl_ref, pm_ref, pl_ref, s0_ref, mum_ref, mul_ref, w0_ref, a0_ref,
                 kk_ref, ka_ref, rk_ref, lng_ref, lnb_ref, ww_ref, wa_ref, wg_ref, ones_ref,
                 o_ref, sout_ref, cm_sc, cl_sc, h_sc, *, g_blk, t_blk, chunk, d_rw):
    c = pl.program_id(1)
    n_pairs = d_rw // RW_PAIR
    prow = lax.broadcasted_iota(jnp.int32, (RW_PAIR, RW_PAIR), 0)
    pcol = lax.broadcasted_iota(jnp.int32, (RW_PAIR, RW_PAIR), 1)
    same_head = (prow < RW_HEAD) == (pcol < RW_HEAD)

    @pl.when(c == 0)
    def _():
        cm_sc[...] = pm_ref[...]
        cl_sc[...] = pl_ref[...]
        for g in range(g_blk):
            for p in range(n_pairs):
                pair = s0_ref[g, 2 * p:2 * p + 2].reshape(RW_PAIR, RW_HEAD)
                kv = pair.T
                h_sc[g, p] = jnp.where(same_head, jnp.concatenate([kv, kv], axis=0), 0.0)

    def shift_mix(z_ref, carry_ref, mu):
        out = []
        for g in range(g_blk):
            z = z_ref[g]
            rows = lax.broadcasted_iota(jnp.int32, z.shape, 0)
            zs = jnp.where(rows == 0, carry_ref[g], pltpu.roll(z, 1, axis=0))
            carry_ref[g] = z[t_blk - 1:t_blk, :]
            out.append(z + (zs - z) * mu)
        return jnp.concatenate(out, axis=0) if g_blk > 1 else out[0]

    zmix = shift_mix(zm_ref, cm_sc, mum_ref[...])
    zlm = shift_mix(zl_ref, cl_sc, mul_ref[...])
    r, k, v = zmix[:, :d_rw], zmix[:, d_rw:2 * d_rw], zmix[:, 2 * d_rw:]

    ones_bd = ones_ref[...]
    n_wa = ww_ref.shape[0]
    z_wa, z_g = zlm[:, :n_wa], zlm[:, n_wa:]
    u = w0_ref[...] + _dot(jnp.tanh(z_wa).astype(BF16), ww_ref[...])
    softplus = jnp.maximum(-u, 0.0) + jnp.log(1.0 + jnp.exp(-jnp.abs(u)))
    lw = -jnp.exp(-softplus - 0.5)
    a = _sigmoid(a0_ref[...] + _dot(z_wa.astype(BF16), wa_ref[...]))
    gate = _dot(_sigmoid(z_g).astype(BF16), wg_ref[...])
    kk = k * kk_ref[...]
    kk = kk / jnp.maximum(jnp.sqrt(_seg_sum(kk * kk, ones_bd)), 1e-12)
    k2 = k * (1.0 + (a - 1.0) * ka_ref[...])
    av = -kk
    bv = kk * a
    bonus = _seg_sum(r * k2 * rk_ref[...], ones_bd) * v

    row = lax.broadcasted_iota(jnp.int32, (chunk, chunk), 0)
    col = lax.broadcasted_iota(jnp.int32, (chunk, chunk), 1)
    tri = jnp.where(row >= col, 1.0, 0.0).astype(BF16)
    zpad = jnp.zeros((chunk - t_blk, d_rw), F32) if chunk > t_blk else None

    def seq_rows(x, g):
        xg = x[g * t_blk:(g + 1) * t_blk]
        return xg if zpad is None else jnp.concatenate([xg, zpad], axis=0)

    lane = lax.broadcasted_iota(jnp.int32, (1, RW_PAIR), 1)
    first = lane < RW_HEAD
    row2 = lax.broadcasted_iota(jnp.int32, (2 * chunk, chunk), 0)
    col2 = lax.broadcasted_iota(jnp.int32, (2 * chunk, chunk), 1)
    mask2 = jnp.where(row2 < chunk, row2, row2 - chunk + 1) > col2
    zero_b = jnp.zeros((), BF16)
    upd_rows = -(-2 * chunk // RW_PAIR) * RW_PAIR
    upd_pad = jnp.zeros((upd_rows - 2 * chunk, RW_PAIR), F32)

    def chunk_pair(g, p, at, rt, bt, kt, bh, kh, v_c, vb, gl):
        sl = slice(p * RW_PAIR, (p + 1) * RW_PAIR)
        lhs = jnp.concatenate([at[:, sl], rt[:, sl]], axis=0)
        h_old = h_sc[g, p]
        hh = _dot(lhs, h_old.astype(BF16))
        ab, ak = [], []
        for e in range(2):
            sel = first if e == 0 else jnp.logical_not(first)
            le = jnp.where(sel, lhs, zero_b)
            ab.append(jnp.where(mask2, _dot_nt(le, bt[:, sl]), 0.0))
            ak.append(jnp.where(mask2, _dot_nt(le, kt[:, sl]), 0.0))
        yield
        xk = [_dot(x.astype(BF16), vb[:, sl]) for x in ak]
        xb = [x[chunk:].astype(BF16) for x in ab]
        tinv = yield from _tri_inverse([x[:chunk] for x in ab], chunk)
        xkv = jnp.where(first, xk[0], xk[1])
        rhs = (hh[:chunk] + xkv[:chunk]).astype(BF16)
        yield
        uu = jnp.where(first, _dot(tinv[0], rhs), _dot(tinv[1], rhs))
        ub = uu.astype(BF16)
        yield
        y = hh[chunk:] + xkv[chunk:] + jnp.where(first, _dot(xb[0], ub), _dot(xb[1], ub))
        pieces_l = [bh[:, sl], kh[:, sl]]
        pieces_r = [uu, v_c[:, sl]]
        if upd_rows > 2 * chunk:
            pieces_l.append(upd_pad)
            pieces_r.append(upd_pad)
        lhs_t = jnp.concatenate(pieces_l, axis=0).T.astype(BF16)
        upd = _dot(lhs_t, jnp.concatenate(pieces_r, axis=0).astype(BF16))
        w_all = jnp.exp(jnp.broadcast_to(gl[:, sl], (RW_PAIR, RW_PAIR))).T
        h_sc[g, p] = h_old * w_all + jnp.where(same_head, upd, 0.0)
        return y[:t_blk]

    tasks = []
    for g in range(g_blk):
        lw_g = seq_rows(lw, g)
        gc = functools.reduce(lambda x, y: x + y, [_dot(tri, part) for part in _split3(lw_g)])
        gl = gc[chunk - 1:chunk, :]
        w_inv = jnp.exp(-gc)
        w_rem = jnp.exp(gl - gc)
        r_g, k_g, v_g, a_g, b_g = (seq_rows(x, g) for x in (r, k2, v, av, bv))
        at = (a_g * jnp.exp(gc - lw_g)).astype(BF16)
        rt = (r_g * jnp.exp(gc)).astype(BF16)
        bt = (b_g * w_inv).astype(BF16)
        kt = (k_g * w_inv).astype(BF16)
        bh = b_g * w_rem
        kh = k_g * w_rem
        vb = v_g.astype(BF16)
        tasks += [chunk_pair(g, p, at, rt, bt, kt, bh, kh, v_g, vb, gl) for p in range(n_pairs)]
    ys = _round_robin(tasks)
    y_rows = [jnp.concatenate(ys[g * n_pairs:(g + 1) * n_pairs], axis=1) for g in range(g_blk)]

    y = jnp.concatenate(y_rows, axis=0) if g_blk > 1 else y_rows[0]
    inv_n = 1.0 / RW_HEAD
    mean = _seg_sum(y, ones_bd) * inv_n
    yc = y - mean
    var = _seg_sum(yc * yc, ones_bd) * inv_n
    o = yc * lax.rsqrt(var + GN_EPS) * lng_ref[...] + lnb_ref[...]
    out = ((o + bonus) * gate).astype(o_ref.dtype)
    for g in range(g_blk):
        o_ref[g] = out[g * t_blk:(g + 1) * t_blk]

    @pl.when(c == pl.num_programs(1) - 1)
    def _():
        for g in range(g_blk):
            for p in range(n_pairs):
                hm = jnp.where(same_head, h_sc[g, p], 0.0)
                kv = hm[:RW_HEAD] + hm[RW_HEAD:]
                sout_ref[g, 2 * p:2 * p + 2] = kv.T.reshape(2, RW_HEAD, RW_HEAD)


def _rwkv(zm, zl, prev_m, prev_l, s0, prm, *, batch, seq, g_blk, t_blk, chunk):
    n, w_main = zm.shape
    d_rw = w_main // 3
    n_lora = zl.shape[1]
    n_pairs = d_rw // RW_PAIR
    row = lambda w: pl.BlockSpec((g_blk, t_blk, w), lambda b, c: (b, c, 0))
    per_b = lambda w: pl.BlockSpec((g_blk, 1, w), lambda b, c: (b, 0, 0))
    st_spec = pl.BlockSpec((g_blk, 2 * n_pairs, RW_HEAD, RW_HEAD), lambda b, c: (b, 0, 0, 0))
    vecs = [prm[name] for name in ("mu_m", "mu_l", "w0", "a0", "k_k", "k_a", "r_k", "ln_g", "ln_b")]
    mats = [prm["ww"], prm["wa"], prm["wg"], prm["ones_seg"]]
    o_b, s_new = pl.pallas_call(
        functools.partial(_rwkv_kernel, g_blk=g_blk, t_blk=t_blk, chunk=chunk, d_rw=d_rw),
        grid=(batch // g_blk, seq // t_blk),
        in_specs=[row(w_main), row(n_lora), per_b(w_main), per_b(n_lora), st_spec]
        + [_const_spec(x.shape) for x in vecs + mats],
        out_specs=[row(d_rw), st_spec],
        out_shape=[jax.ShapeDtypeStruct((batch, seq, d_rw), F32),
                   jax.ShapeDtypeStruct(s0.shape, F32)],
        scratch_shapes=[pltpu.VMEM((g_blk, 1, w_main), F32), pltpu.VMEM((g_blk, 1, n_lora), F32),
                        pltpu.VMEM((g_blk, n_pairs, RW_PAIR, RW_PAIR), F32)],
        compiler_params=_params("arbitrary", "arbitrary"),
        name="rwkv_mix",
    )(zm.reshape(batch, seq, w_main), zl.reshape(batch, seq, n_lora),
      prev_m.reshape(batch, 1, w_main), prev_l.reshape(batch, 1, n_lora), s0, *vecs, *mats)
    return o_b.reshape(n, d_rw), s_new


def _ffn_kernel(x_ref, oa_ref, ob_ref, cp_ref, woa_ref, wob_ref, nf_ref, wup_ref, cw_ref,
                cb_ref, wdn_ref, nfin_ref, y_ref, cs_ref, carry_sc, *, d_ff, cw, carried,
                apply_final):
    t = pl.program_id(1)
    tm = x_ref.shape[0]
    groups = tm // SUBLANES

    if carried:
        @pl.when(t == 0)
        def _():
            carry_sc[...] = jnp.zeros(carry_sc.shape, F32)
            carry_sc[:, SUBLANES - (CONV_W - 1):, :] = cp_ref[...]

    x1 = (x_ref[...] + _dot(oa_ref[...].astype(BF16), woa_ref[...])
          + _dot(ob_ref[...].astype(BF16), wob_ref[...]))
    h = _rmsnorm(x1, nf_ref[...], RMS_EPS).astype(BF16)
    t8 = lax.broadcasted_iota(jnp.int32, (groups, SUBLANES, cw), 1)

    def up_proj(ci):
        return [_dot(h, wup_ref[:, col0:col0 + cw]) for col0 in (ci * cw, d_ff + ci * cw)]

    def conv(up, col0):
        cols = slice(col0, col0 + cw)
        up = up.reshape(groups, SUBLANES, cw)
        if carried:
            prev = carry_sc[:, :, cols]
            if groups > 1:
                prev = jnp.concatenate([prev, up[:-1]], axis=0)
            p6, p7 = prev[:, 6:7, :], prev[:, 7:8, :]
            carry_sc[:, :, cols] = up[groups - 1:]
            cs_ref[:, :, cols] = up[groups - 1:, SUBLANES - (CONV_W - 1):, :]
        else:
            p6, p7 = cp_ref[:, 0:1, cols], cp_ref[:, 1:2, cols]
            cs_ref[:, :, cols] = up[:, SUBLANES - (CONV_W - 1):, :]
        m1 = jnp.where(t8 == 0, p7, pltpu.roll(up, 1, axis=1))
        m2 = jnp.where(t8 == 0, p6, jnp.where(t8 == 1, p7, pltpu.roll(up, 2, axis=1)))
        w = cw_ref[:, cols]
        out = cb_ref[:, cols] + m2 * w[0:1] + m1 * w[1:2] + up * w[2:3]
        return out.reshape(tm, cw)

    n_chunks = d_ff // cw
    acc = jnp.zeros((tm, x_ref.shape[1]), F32)
    ups = up_proj(0)
    for ci in range(n_chunks):
        nxt = up_proj(ci + 1) if ci + 1 < n_chunks else None
        gate = conv(ups[0], ci * cw)
        val = conv(ups[1], d_ff + ci * cw)
        act = (gate * _sigmoid(gate) * val).astype(BF16)
        acc = acc + _dot(act, wdn_ref[ci * cw:(ci + 1) * cw, :])
        ups = nxt
    x2 = x1 + acc
    y_ref[...] = _rmsnorm(x2, nfin_ref[...], RMS_EPS) if apply_final else x2


def _ffn(x, oa, ob, conv_prev, woa, wob, nf, wup, cwt, cb, wdn, nfin, *, batch, seq, tm, cw,
         apply_final):
    n, d = x.shape
    d_mix = oa.shape[1]
    d_ff2 = wup.shape[1]
    d_ff = d_ff2 // 2
    carried = seq >= tm
    if carried:
        nt = seq // tm
        grid = (batch, nt)
        row = lambda w: pl.BlockSpec((tm, w), lambda b, t: (b * nt + t, 0))
        cs_spec = pl.BlockSpec((1, CONV_W - 1, d_ff2), lambda b, t: (b, 0, 0))
        cp_spec = cs_spec
    else:
        seqs = tm // seq
        grid = (batch // seqs, 1)
        row = lambda w: pl.BlockSpec((tm, w), lambda b, t: (b, 0))
        cs_spec = pl.BlockSpec((seqs, CONV_W - 1, d_ff2), lambda b, t: (b, 0, 0))
        cp_spec = cs_spec
    return pl.pallas_call(
        functools.partial(_ffn_kernel, d_ff=d_ff, cw=cw, carried=carried,
                          apply_final=apply_final),
        grid=grid,
        in_specs=[row(d), row(d_mix), row(d_mix), cp_spec, _const_spec(woa.shape),
                  _const_spec(wob.shape), _const_spec(nf.shape), _const_spec(wup.shape),
                  _const_spec(cwt.shape), _const_spec(cb.shape), _const_spec(wdn.shape),
                  _const_spec(nfin.shape)],
        out_specs=[row(d), cs_spec],
        out_shape=[jax.ShapeDtypeStruct((n, d), F32),
                   jax.ShapeDtypeStruct((batch, CONV_W - 1, d_ff2), F32)],
        scratch_shapes=[pltpu.VMEM((1, SUBLANES, d_ff2), F32)],
        compiler_params=_params("arbitrary", "arbitrary"),
        name="out_proj_ffn",
    )(x, oa, ob, conv_prev, woa, wob, nf, wup, cwt, cb, wdn, nfin)


def _pick_tile(n, target):
    t = min(n, target)
    while n % t:
        t //= 2
    return t


def kernel(x_prompt, x_sample, cache_k, cache_v, state_wkv, state_shift, state_conv, page_table, norm_mix, w_in, lam_q1, lam_k1, lam_q2, lam_k2, subln, rw_mu, rw_w0, rw_w_up, rw_a0, rw_a_up, rw_g_up, rw_k_k, rw_k_a, rw_r_k, rw_ln_g, rw_ln_b, w_out, norm_ffn, w_up, conv_w, conv_b, w_down, norm_final):
    depth = w_in.shape[0]
    bp, tp, d = x_prompt.shape
    bs, ts, _ = x_sample.shape
    d_rw = rw_w0.shape[1]
    d_diff = DA_HEADS * DA_V
    n_wl, n_al, n_gl = rw_w_up.shape[1], rw_a_up.shape[1], rw_g_up.shape[1]
    n_lora = n_wl + n_al + n_gl
    n_rw_cols = 3 * d_rw + n_lora

    xp = x_prompt.reshape(bp * tp, d)
    xs = x_sample.reshape(bs * ts, d)
    assert (n_wl + n_al) % LANES == 0 and d_rw % (2 * RW_PAIR) == 0
    head_of = jnp.arange(2 * RW_PAIR) // RW_HEAD
    ones_seg = (head_of[:, None] == head_of[None, :]).astype(BF16)
    row2 = lambda vec: vec.reshape(1, -1)

    outs_p = [[] for _ in range(5)]
    outs_s = [[] for _ in range(5)]
    for layer in range(depth):
        lam_init = 0.8 - 0.6 * math.exp(-0.3 * layer)
        last = layer == depth - 1
        w_in_b = w_in[layer].astype(BF16)
        lamv = jnp.stack([lam_q1[layer], lam_k1[layer], lam_q2[layer], lam_k2[layer]])
        sub = row2(subln[layer])
        mu = rw_mu[layer]
        zpad = lambda rows: jnp.zeros((rows, d_rw), BF16)
        prm = {
            "mu_m": row2(mu[:3 * d_rw]), "mu_l": row2(mu[3 * d_rw:]),
            "w0": row2(rw_w0[layer]), "a0": row2(rw_a0[layer]),
            "k_k": row2(rw_k_k[layer]), "k_a": row2(rw_k_a[layer]),
            "r_k": row2(rw_r_k[layer]), "ln_g": row2(rw_ln_g[layer]), "ln_b": row2(rw_ln_b[layer]),
            "ww": jnp.concatenate([rw_w_up[layer].astype(BF16), zpad(n_al)], axis=0),
            "wa": jnp.concatenate([zpad(n_wl), rw_a_up[layer].astype(BF16)], axis=0),
            "wg": rw_g_up[layer].astype(BF16),
            "ones_seg": ones_seg,
        }
        wo = w_out[layer].astype(BF16)
        ffn_w = (wo[:d_diff], wo[d_diff:], row2(norm_ffn[layer]), w_up[layer].astype(BF16),
                 conv_w[layer], row2(conv_b[layer]), w_down[layer].astype(BF16), row2(norm_final))

        def run(x, batch, seq, prev_shift, s0, conv_prev, attn_fn):
            n = batch * seq
            q, k, v, kb, vb, zm, zl = _norm_proj(x, row2(norm_mix[layer]), w_in_b, d_diff=d_diff,
                                                 n_rkv=3 * d_rw, tm=_pick_tile(n, 512),
                                                 q_dtype=BF16 if seq % 16 == 0 else F32)
            o_a = attn_fn(q, k, v, kb, vb)
            t_blk = min(seq, 64)
            o_b, s_new = _rwkv(zm, zl, prev_shift[:, :3 * d_rw], prev_shift[:, 3 * d_rw:],
                               s0, prm, batch=batch, seq=seq,
                               g_blk=_pick_tile(batch, 4 if t_blk == 64 else 8),
                               t_blk=t_blk, chunk=max(t_blk, 16))
            tm = _pick_tile(n, 512 if seq >= 512 else 256)
            y, conv_new = _ffn(x, o_a, o_b, conv_prev, *ffn_w, batch=batch, seq=seq, tm=tm,
                               cw=2816, apply_final=last)
            shift_new = jnp.concatenate([zm.reshape(batch, seq, -1)[:, -1],
                                         zl.reshape(batch, seq, -1)[:, -1]], axis=-1)
            return (y, k.reshape(batch, seq, DA_HEADS, DA_V), v.reshape(batch, seq, DA_HEADS, DA_V),
                    s_new, shift_new, conv_new)

        prompt_attn = lambda q, k, v, kb, vb: _attn_prompt(
            lamv, sub, q, kb, vb, batch=bp, seq=tp, tq=_pick_tile(tp, 256), lam_init=lam_init)
        xp, kp, vp, wp, sp, cp = run(
            xp, bp, tp, jnp.zeros((bp, n_rw_cols), F32),
            jnp.zeros((bp, d_rw // RW_HEAD, RW_HEAD, RW_HEAD), F32),
            jnp.zeros((bp, CONV_W - 1, w_up.shape[2]), F32), prompt_attn)
        sample_attn = lambda q, k, v, kb, vb: _attn_paged(
            page_table, lamv, sub, q, k, v, cache_k, cache_v,
            pages_per_step=_pick_tile(page_table.shape[1], 16),
            pool_offset=layer * cache_k.shape[1], lam_init=lam_init)
        xs, ks, vs, ws, ss, cs = run(
            xs, bs, ts, state_shift[layer], state_wkv[layer],
            state_conv[layer], sample_attn)
        for acc, val in zip(outs_p, (kp, vp, wp, sp, cp)):
            acc.append(val)
        for acc, val in zip(outs_s, (ks, vs, ws, ss, cs)):
            acc.append(val)

    y_prompt = xp.reshape(bp, tp, d)
    y_sample = xs.reshape(bs, ts, d)
    return (y_prompt, y_sample, *[jnp.stack(o) for o in outs_p], *[jnp.stack(o) for o in outs_s])
```
